```python
import math
import jax, jax.numpy as jnp
from jax import lax
import numpy as np

D_MODEL = 1024
BATCH = 2
SEQ = 16384
DEPTH = 2

GRID_W = 64
CTX_LEN = 256
GDN_DK = 128
GDN_DV = 128
GDN_W = D_MODEL // 2
GDN_HEADS = GDN_W // GDN_DV
FNET_W = D_MODEL // 4
FNET_GROUPS = 4
FNET_GDIM = FNET_W // FNET_GROUPS
CONV_W = D_MODEL // 4
CONV_GROUPS = 4
MIX_W = GDN_W + FNET_W + CONV_W
SHORT_CONV = 3
CONV_K = 31
CHUNK = 64
QKV_OFF = 0
Z_OFF = 3 * GDN_W
A_OFF = Z_OFF + GDN_W
B_OFF = A_OFF + 2 * GDN_HEADS
F_OFF = B_OFF + 2 * GDN_HEADS
C_OFF = F_OFF + FNET_W
IN_W = C_OFF + 2 * CONV_W
PEER_HEADS = 8
PEER_NKEYS = 128
PEER_EXPERTS = PEER_NKEYS * PEER_NKEYS
PEER_TOPK = 16
PEER_DQ = 256
PEER_DH = PEER_DQ // 2
PEER_BLOCK = 128
EPS = 1e-6
NEG = -1e30

kernel_name = "hybrid_gdn_fnet_conformer_peer_dit"


def _rms(x, w):
    xf = x.astype(jnp.float32)
    y = xf * lax.rsqrt(jnp.mean(xf * xf, axis=-1, keepdims=True) + EPS)
    return (y * w.astype(jnp.float32)).astype(x.dtype)


def _l2n(t):
    return t * lax.rsqrt(jnp.sum(t * t, axis=-1, keepdims=True) + EPS)


def _dwconv(x, w, pad):
    return lax.conv_general_dilated(
        x, w[:, None, :].astype(x.dtype), (1,), [(pad, pad)],
        dimension_numbers=("NWC", "WIO", "NWC"), feature_group_count=x.shape[-1])


def _gdn_chunked(q, k, v, g, beta, s0):
    b, h, L, _ = q.shape
    dv = v.shape[-1]
    n = L // CHUNK
    blk = lambda t: t.reshape(b, h, n, CHUNK, *t.shape[3:])
    q, k, v, g, beta = blk(q), blk(k), blk(v), blk(g), blk(beta)
    g = jnp.cumsum(g, axis=-1)
    idx = jnp.arange(CHUNK)
    incl = idx[:, None] >= idx[None, :]
    strict = idx[:, None] > idx[None, :]
    decay = jnp.exp(jnp.where(incl, g[..., :, None] - g[..., None, :], NEG))
    kb = k * beta[..., None]
    a_mat = jnp.where(strict, jnp.einsum("bhnid,bhnjd->bhnij", kb, k) * decay, 0.0)
    rhs = jnp.concatenate([v * beta[..., None], kb * jnp.exp(g)[..., None]], axis=-1)
    sol = lax.linalg.triangular_solve(a_mat + jnp.eye(CHUNK, dtype=jnp.float32), rhs,
                                      left_side=True, lower=True, unit_diagonal=True)
    u, w = sol[..., :dv], sol[..., dv:]
    qk = jnp.einsum("bhnid,bhnjd->bhnij", q, k) * decay
    g_last = g[..., -1]
    k_tail = k * jnp.exp(g_last[..., None] - g)[..., None]
    q_dec = q * jnp.exp(g)[..., None]

    def step(s, inp):
        q_i, k_i, u_i, w_i, qk_i, gl_i = inp
        v_new = u_i - jnp.einsum("bhck,bhkv->bhcv", w_i, s)
        o_i = jnp.einsum("bhck,bhkv->bhcv", q_i, s) + jnp.einsum("bhcj,bhjv->bhcv", qk_i, v_new)
        s = s * jnp.exp(gl_i)[..., None, None] + jnp.einsum("bhck,bhcv->bhkv", k_i, v_new)
        return s, o_i

    mv = lambda t: jnp.moveaxis(t, 2, 0)
    s_fin, o = lax.scan(step, s0, (mv(q_dec), mv(k_tail), mv(u), mv(w), mv(qk), mv(g_last)))
    o = jnp.moveaxis(o, 0, 2).reshape(b, h, L, dv)
    return o, s_fin


def _gdn_bidir(q, k, v, g, beta, s0_f, s0_b):
    o_f, s_f = _gdn_chunked(q, k, v, g[0], beta[0], s0_f)
    fl = lambda t: jnp.flip(t, axis=2)
    o_b, s_b = _gdn_chunked(fl(q), fl(k), fl(v), fl(g[1]), fl(beta[1]), s0_b)
    return o_f + fl(o_b), s_f, s_b


def _gdn_inputs(p, conv_qkv, a_log, dt_bias):
    bsz, L, _ = p.shape
    qkv = jax.nn.silu(_dwconv(p[..., QKV_OFF:Z_OFF], conv_qkv, SHORT_CONV // 2)).astype(jnp.float32)
    heads = lambda t: t.reshape(bsz, L, GDN_HEADS, -1).transpose(0, 2, 1, 3)
    q, k, v = (heads(t) for t in jnp.split(qkv, 3, axis=-1))
    q = _l2n(q) * (GDN_DK ** -0.5)
    k = _l2n(k)
    a = p[..., A_OFF:B_OFF].astype(jnp.float32).reshape(bsz, L, 2, GDN_HEADS)
    bb = p[..., B_OFF:F_OFF].astype(jnp.float32).reshape(bsz, L, 2, GDN_HEADS)
    g = -jnp.exp(a_log.astype(jnp.float32)) * jax.nn.softplus(a + dt_bias.astype(jnp.float32))
    beta = jax.nn.sigmoid(bb)
    return q, k, v, g.transpose(2, 0, 3, 1), beta.transpose(2, 0, 3, 1)


def _gdn_out(o, z, w_norm):
    bsz, _, L, _ = o.shape
    o = o.transpose(0, 2, 1, 3)
    o = o * lax.rsqrt(jnp.mean(o * o, axis=-1, keepdims=True) + EPS) * w_norm.astype(jnp.float32)
    zf = z.astype(jnp.float32).reshape(bsz, L, GDN_HEADS, GDN_DV)
    return (o * jax.nn.silu(zf)).reshape(bsz, L, GDN_W).astype(z.dtype)


def _fnet(f):
    bsz, L, _ = f.shape
    ff = f.astype(jnp.float32).reshape(bsz, L, FNET_GROUPS, FNET_GDIM)
    y = jnp.fft.fft2(ff, axes=(1, 3), norm="ortho").real
    return y.reshape(bsz, L, FNET_W).astype(f.dtype)


def _conv_module(gl, dw_w, dw_b, gn_w, gn_b, rows):
    bsz, L, _ = gl.shape
    y = gl[..., :CONV_W] * jax.nn.sigmoid(gl[..., CONV_W:])
    if rows is not None:
        y = y.reshape(bsz * rows, GRID_W, CONV_W)
    y = (_dwconv(y, dw_w, CONV_K // 2) + dw_b.astype(y.dtype)).reshape(bsz, L, CONV_W)
    yf = y.astype(jnp.float32).reshape(bsz, L, CONV_GROUPS, CONV_W // CONV_GROUPS)
    mu = jnp.mean(yf, axis=-1, keepdims=True)
    var = jnp.mean(jnp.square(yf - mu), axis=-1, keepdims=True)
    yf = ((yf - mu) * lax.rsqrt(var + EPS)).reshape(bsz, L, CONV_W)
    yf = yf * gn_w.astype(jnp.float32) + gn_b.astype(jnp.float32)
    return jax.nn.silu(yf).astype(gl.dtype)


def _mix_out(p, o, gdn_norm, dw_w, dw_b, gn_w, gn_b, w_out, rows):
    y_gdn = _gdn_out(o, p[..., Z_OFF:A_OFF], gdn_norm)
    y_f = _fnet(p[..., F_OFF:C_OFF])
    y_c = _conv_module(p[..., C_OFF:IN_W], dw_w, dw_b, gn_w, gn_b, rows)
    return jnp.concatenate([y_gdn, y_f, y_c], axis=-1) @ w_out


def _peer(h, wq, keys, u_tab, v_tab):
    bsz, L, D = h.shape
    T = bsz * L
    hf = h.reshape(T, D)
    qr = (hf @ wq).reshape(T, PEER_HEADS, 2, PEER_DH)
    s = jnp.einsum("thpd,hpkd->thpk", qr, keys).astype(jnp.float32)
    sv, si = lax.top_k(s, PEER_TOPK)
    cand = (sv[:, :, 0, :, None] + sv[:, :, 1, None, :]).reshape(T, PEER_HEADS, -1)
    cid = (si[:, :, 0, :, None] * PEER_NKEYS + si[:, :, 1, None, :]).reshape(T, PEER_HEADS, -1)
    fv, fi = lax.top_k(cand, PEER_TOPK)
    eid = jnp.take_along_axis(cid, fi, axis=-1).reshape(T, -1)
    gw = jax.nn.softmax(fv, axis=-1).astype(h.dtype).reshape(T, -1)
    nb = T // PEER_BLOCK

    def blk(args):
        hb, eb, gb = args
        act = jax.nn.gelu(jnp.einsum("tkd,td->tk", u_tab[eb], hb))
        return jnp.einsum("tk,tkd->td", gb * act, v_tab[eb])

    out = lax.map(blk, (hf.reshape(nb, PEER_BLOCK, D),
                        eid.reshape(nb, PEER_BLOCK, -1),
                        gw.reshape(nb, PEER_BLOCK, -1)))
    return out.reshape(bsz, L, D)


def _layer(x, xc, mod, mod_c, rows, last, norm1, norm2, w_in, conv_qkv, a_log, dt_bias,
           gdn_norm, dw_w, dw_b, gn_w, gn_b, w_out, peer_wq, peer_keys, peer_u, peer_v):
    sh1, sc1, gt1, sh2, sc2, gt2 = (mod[:, j][:, None, :] for j in range(6))
    csh1, csc1, cgt1, csh2, csc2, cgt2 = (mod_c[j] for j in range(6))
    h = _rms(x, norm1) * (1 + sc1) + sh1
    hc = _rms(xc, norm1) * (1 + csc1) + csh1
    p = h @ w_in
    pc = hc @ w_in
    zero = jnp.zeros((x.shape[0], GDN_HEADS, GDN_DK, GDN_DV), jnp.float32)
    oc, s_f, s_b = _gdn_bidir(*_gdn_inputs(pc, conv_qkv, a_log, dt_bias), zero, zero)
    o, _, _ = _gdn_bidir(*_gdn_inputs(p, conv_qkv, a_log, dt_bias), s_f, s_b)
    mix = lambda pp, oo, r: _mix_out(pp, oo, gdn_norm, dw_w, dw_b, gn_w, gn_b, w_out, r)
    peer = lambda hh: _peer(hh, peer_wq, peer_keys, peer_u, peer_v)
    x = x + gt1 * mix(p, o, rows)
    x = x + gt2 * peer(_rms(x, norm2) * (1 + sc2) + sh2)
    if not last:
        xc = xc + cgt1 * mix(pc, oc, None)
        xc = xc + cgt2 * peer(_rms(xc, norm2) * (1 + csc2) + csh2)
    return x, xc


def setup_inputs(seed: int = 0) -> dict:
    key = jax.random.key(seed)
    ks = jax.random.split(key, 24)
    D = D_MODEL
    nrm = lambda k, shape, s: jax.random.normal(k, shape, jnp.float32) * s
    dt = jnp.exp(jax.random.uniform(ks[11], (DEPTH, 2, GDN_HEADS), jnp.float32,
                                    math.log(1e-3), math.log(1e-1)))
    return {
        "x": nrm(ks[0], (BATCH, SEQ, D), 1.0),
        "c": nrm(ks[1], (BATCH, D), 1.0),
        "ctx": nrm(ks[2], (BATCH, CTX_LEN, D), 1.0),
        "c_ctx": nrm(ks[3], (D,), 1.0),
        "w_mod": nrm(ks[4], (DEPTH, D, 6 * D), 0.5 * D ** -0.5),
        "b_mod": nrm(ks[5], (DEPTH, 6 * D), 0.01),
        "norm1": 1.0 + nrm(ks[6], (DEPTH, D), 0.02),
        "norm2": 1.0 + nrm(ks[7], (DEPTH, D), 0.02),
        "w_in": nrm(ks[8], (DEPTH, D, IN_W), D ** -0.5),
        "conv_qkv": nrm(ks[9], (DEPTH, SHORT_CONV, 3 * GDN_W), SHORT_CONV ** -0.5),
        "a_log": jnp.log(jax.random.uniform(ks[10], (DEPTH, 2, GDN_HEADS), jnp.float32, 1.0, 16.0)),
        "dt_bias": dt + jnp.log(-jnp.expm1(-dt)),
        "gdn_norm": 1.0 + nrm(ks[12], (DEPTH, GDN_DV), 0.02),
        "dw_w": nrm(ks[13], (DEPTH, CONV_K, CONV_W), CONV_K ** -0.5),
        "dw_b": nrm(ks[14], (DEPTH, CONV_W), 0.02),
        "gn_w": 1.0 + nrm(ks[15], (DEPTH, CONV_W), 0.02),
        "gn_b": nrm(ks[16], (DEPTH, CONV_W), 0.02),
        "w_out": nrm(ks[17], (DEPTH, MIX_W, D), MIX_W ** -0.5),
        "peer_wq": nrm(ks[18], (DEPTH, D, PEER_HEADS * PEER_DQ), D ** -0.5),
        "peer_keys": nrm(ks[19], (DEPTH, PEER_HEADS, 2, PEER_NKEYS, PEER_DH), PEER_DH ** -0.5),
        "peer_u": nrm(ks[20], (DEPTH, PEER_EXPERTS, D), D ** -0.5),
        "peer_v": nrm(ks[21], (DEPTH, PEER_EXPERTS, D), 1.0),
        "final_norm": 1.0 + nrm(ks[22], (D,), 0.02),
    }


def reference(x, c, ctx, c_ctx, w_mod, b_mod, norm1, norm2, w_in, conv_qkv, a_log, dt_bias,
              gdn_norm, dw_w, dw_b, gn_w, gn_b, w_out, peer_wq, peer_keys, peer_u, peer_v,
              final_norm):
    rows = x.shape[1] // GRID_W
    xc = ctx
    for i in range(DEPTH):
        mod = (jax.nn.silu(c) @ w_mod[i] + b_mod[i]).reshape(c.shape[0], 6, D_MODEL)
        mod_c = (jax.nn.silu(c_ctx) @ w_mod[i] + b_mod[i]).reshape(6, D_MODEL)
        x, xc = _layer(x, xc, mod, mod_c, rows, i == DEPTH - 1,
                       norm1[i], norm2[i], w_in[i], conv_qkv[i], a_log[i], dt_bias[i],
                       gdn_norm[i], dw_w[i], dw_b[i], gn_w[i], gn_b[i], w_out[i],
                       peer_wq[i], peer_keys[i], peer_u[i], peer_v[i])
    return _rms(x, final_norm)
```

```python
import functools
import math

import numpy as np
import jax
import jax.numpy as jnp
from jax import lax
from jax.experimental import pallas as pl
from jax.experimental.pallas import tpu as pltpu

f32 = jnp.float32
bf16 = jnp.bfloat16

D_MODEL = 1024
DEPTH = 2
GRID_W = 64
GDN_DK = 128
GDN_HEADS = 4
GDN_W = 512
FNET_W = 256
FNET_GROUPS = 4
FNET_GDIM = 64
CONV_W = 256
CONV_GROUPS = 4
CONV_K = 31
QKV_OFF, Z_OFF, A_OFF, B_OFF, F_OFF, C_OFF, IN_W = 0, 1536, 2048, 2056, 2064, 2320, 2832
PEER_HEADS = 8
PEER_NKEYS = 128
PEER_TOPK = 16
PEER_DH = 128
EPS = 1e-6

LANES = 128
SUBLANES = 8
VMEM_LIMIT_BYTES = 56 * 2**20

GDN_CHUNK = 128
GDN_BLOCK = 256
GDN_SUB = 16
ROW_TILE = 256
CONV_TILE = 512
FNET_LANE_TILE = 8 * FNET_W
FNET_SHORT_MAX = 512
PEER_TOKEN_TILE = 512
PEER_EXPERT_TILE = 1024
PEER_TOPK_TILE = 256
PEER_LANE_CHUNK = 256
NEG_INF = float("-inf")


def _params(*sem):
    return pltpu.CompilerParams(dimension_semantics=sem, vmem_limit_bytes=VMEM_LIMIT_BYTES)


def _dot(a, b):
    return jnp.dot(a.astype(bf16), b.astype(bf16), preferred_element_type=f32)


def _dot_nt(a, b):
    return lax.dot_general(a.astype(bf16), b.astype(bf16), (((1,), (1,)), ((), ())),
                           preferred_element_type=f32)


def _dot_f32(a, b):
    return jnp.dot(a, b, preferred_element_type=f32, precision=lax.Precision.HIGHEST)


def _silu(x):
    return x * jax.nn.sigmoid(x)


def _mod_kernel(c_ref, w_ref, b_ref, o_ref):
    o_ref[0] = _dot(_silu(c_ref[...]), w_ref[0]) + b_ref[0]


def _modulation(cc, w_mod, b_mod):
    depth, d, n = w_mod.shape
    tn = n // 4
    return pl.pallas_call(
        _mod_kernel,
        grid=(depth, n // tn),
        in_specs=[pl.BlockSpec((SUBLANES, d), lambda i, j: (0, 0)),
                  pl.BlockSpec((1, d, tn), lambda i, j: (i, 0, j)),
                  pl.BlockSpec((1, 1, tn), lambda i, j: (i, 0, j))],
        out_specs=pl.BlockSpec((1, SUBLANES, tn), lambda i, j: (i, 0, j)),
        out_shape=jax.ShapeDtypeStruct((depth, SUBLANES, n), f32),
        compiler_params=_params("parallel", "parallel"),
        name="modulation",
    )(cc, w_mod, b_mod.reshape(depth, 1, n))


def _ada_norm(x, nw, sc, sh):
    r = lax.rsqrt(jnp.mean(x * x, axis=-1, keepdims=True) + EPS)
    return (x * r * nw) * (1.0 + sc) + sh


IN_SPLITS = (3 * GDN_W, GDN_W, FNET_W, 2 * CONV_W, LANES)


def _norm_in_kernel(x_ref, nw_ref, sc_ref, sh_ref, w_ref, *out_refs):
    hb = _ada_norm(x_ref[0], nw_ref[...], sc_ref[0], sh_ref[0]).astype(bf16)
    off = 0
    for o_ref, width in zip(out_refs, IN_SPLITS):
        o_ref[0] = jnp.dot(hb, w_ref[:, off:off + width], preferred_element_type=f32)
        off += width


def _pack_w_in(w_in):
    ab = jnp.pad(w_in[:, A_OFF:F_OFF], ((0, 0), (0, LANES - (F_OFF - A_OFF))))
    return jnp.concatenate([w_in[:, QKV_OFF:A_OFF], w_in[:, F_OFF:IN_W], ab], axis=1).astype(bf16)


def _norm_in(x, nw, sc, sh, w_packed):
    b, l, d = x.shape
    tm = min(ROW_TILE, l)
    n = w_packed.shape[1]
    vec = pl.BlockSpec((1, 1, d), lambda i, j: (i, 0, 0))
    return pl.pallas_call(
        _norm_in_kernel,
        grid=(b, l // tm),
        in_specs=[pl.BlockSpec((1, tm, d), lambda i, j: (i, j, 0)),
                  pl.BlockSpec((1, d), lambda i, j: (0, 0)),
                  vec, vec,
                  pl.BlockSpec((d, n), lambda i, j: (0, 0))],
        out_specs=[pl.BlockSpec((1, tm, w), lambda i, j: (i, j, 0)) for w in IN_SPLITS],
        out_shape=[jax.ShapeDtypeStruct((b, l, w), f32) for w in IN_SPLITS],
        compiler_params=_params("parallel", "parallel"),
        name="norm_in_proj",
    )(x, nw.reshape(1, d), sc, sh, w_packed)


def _split3(x):
    x1 = x.astype(bf16)
    r1 = x - x1.astype(f32)
    x2 = r1.astype(bf16)
    x3 = (r1 - x2.astype(f32)).astype(bf16)
    return x1, x2, x3


def _dot01(m_ref, parts):
    m = m_ref[...]
    return sum(jnp.dot(m, p, preferred_element_type=f32) for p in parts)


def _gdn_prep_kernel(prev_ref, x_ref, next_ref, ab_ref, cw_ref, alog_ref, dtb_ref,
                     tril_ref, triu_ref, same_ref, q_ref, k_ref, v_ref, gb_ref):
    j = pl.program_id(1)
    tl = x_ref.shape[1]
    first = j == 0
    last = j == pl.num_programs(1) - 1
    row = lax.broadcasted_iota(jnp.int32, (tl, LANES), 0)
    outs = (q_ref, k_ref, v_ref)
    for part in range(3):
        for h in range(GDN_HEADS):
            lo = part * GDN_W + h * GDN_DK
            sl = slice(lo, lo + GDN_DK)
            x = x_ref[0, :, sl]
            prev_row = jnp.where(first, 0.0, prev_ref[0, SUBLANES - 1:SUBLANES, sl])
            next_row = jnp.where(last, 0.0, next_ref[0, 0:1, sl])
            xp = jnp.where(row == 0, prev_row, pltpu.roll(x, 1, axis=0))
            xn = jnp.where(row == tl - 1, next_row, pltpu.roll(x, tl - 1, axis=0))
            y = _silu(cw_ref[0:1, sl] * xp + cw_ref[1:2, sl] * x + cw_ref[2:3, sl] * xn)
            if part < 2:
                y = y * lax.rsqrt(jnp.sum(y * y, axis=-1, keepdims=True) + EPS)
            if part == 0:
                y = y * (GDN_DK ** -0.5)
            outs[part][0, :, h * GDN_DK:(h + 1) * GDN_DK] = y
    ab = ab_ref[0]
    col = lax.broadcasted_iota(jnp.int32, ab.shape, 1)
    nh2 = 2 * GDN_HEADS
    sp = jnp.maximum(ab + dtb_ref[...], 0.0) + jnp.log1p(jnp.exp(-jnp.abs(ab + dtb_ref[...])))
    g = jnp.where(col < nh2, -jnp.exp(alog_ref[...]) * sp, 0.0)
    parts = _split3(g)
    g_f = _dot01(tril_ref, parts)
    g_b = _dot01(triu_ref, parts)
    tot = _dot01(same_ref, parts)
    beta = jax.nn.sigmoid(ab)
    out = jnp.where(col < GDN_HEADS, g_f, jnp.where(col < nh2, g_b, 0.0))
    out = out + jnp.where((col >= nh2) & (col < 2 * nh2), beta, 0.0)
    out = out + pltpu.roll(tot, 2 * nh2, axis=1)
    gb_ref[0] = out


def _chunk_masks(tb, chunk):
    i = np.arange(tb)[:, None]
    j = np.arange(tb)[None, :]
    same = (i // chunk) == (j // chunk)
    return same, i, j


def _gdn_prep(qkv, ab, conv_qkv, a_log, dt_bias):
    b, l, w = qkv.shape
    tl = min(GDN_BLOCK, l)
    nblk = tl // SUBLANES
    same, i, j = _chunk_masks(tl, GDN_CHUNK)
    tril = jnp.asarray(same & (i >= j), bf16)
    triu = jnp.asarray(same & (i <= j), bf16)
    samem = jnp.asarray(same, bf16)
    row = lambda v: jnp.pad(v.reshape(1, -1).astype(f32), ((0, 0), (0, LANES - v.size)))
    nlast = l // SUBLANES - 1
    const = lambda shape: pl.BlockSpec(shape, lambda bi, ji: (0,) * len(shape))
    tok = lambda width: pl.BlockSpec((1, tl, width), lambda bi, ji: (bi, ji, 0))
    return pl.pallas_call(
        _gdn_prep_kernel,
        grid=(b, l // tl),
        in_specs=[pl.BlockSpec((1, SUBLANES, w), lambda bi, ji: (bi, jnp.maximum(ji * nblk - 1, 0), 0)),
                  tok(w),
                  pl.BlockSpec((1, SUBLANES, w), lambda bi, ji: (bi, jnp.minimum((ji + 1) * nblk, nlast), 0)),
                  tok(LANES),
                  const((3, w)), const((1, LANES)), const((1, LANES)),
                  const((tl, tl)), const((tl, tl)), const((tl, tl))],
        out_specs=[tok(GDN_W), tok(GDN_W), tok(GDN_W), tok(LANES)],
        out_shape=[jax.ShapeDtypeStruct((b, l, GDN_W), f32)] * 3 + [jax.ShapeDtypeStruct((b, l, LANES), f32)],
        compiler_params=_params("parallel", "parallel"),
        name="gdn_prep",
    )(qkv, qkv, qkv, ab, conv_qkv, row(a_log), row(dt_bias), tril, triu, samem)


def _neumann_inverse(a, sub_mask, n_sub):
    d = a * sub_mask
    e = a - d
    r = -d
    n = r
    p = r
    for _ in range(int(math.log2(GDN_SUB)) - 1):
        p = _dot(p, p)
        n = n + p + _dot(n, p)
    q = -(e + _dot(n, e))
    y = q
    p = q
    for _ in range(int(math.log2(n_sub)) - 1):
        p = _dot(p, p)
        y = y + p + _dot(y, p)
    return y + n + _dot(y, n)


def _gdn_intra_kernel(q_ref, k_ref, v_ref, gb_ref, m_ref, u_ref, w_ref, qd_ref, qk_ref, kt_ref):
    tb = q_ref.shape[1]
    chunk = min(GDN_CHUNK, tb)
    gb = gb_ref[0]
    gbt = gb.T
    sub_mask = m_ref[4]
    nh2 = 2 * GDN_HEADS
    for h in range(GDN_HEADS):
        sl = slice(h * GDN_DK, (h + 1) * GDN_DK)
        qh, kh, vh = q_ref[0, :, sl], k_ref[0, :, sl], v_ref[0, :, sl]
        kk = _dot_nt(kh, kh)
        qk = _dot_nt(qh, kh)
        kht = kh.T
        for d in range(2):
            c = d * GDN_HEADS + h
            g_col, g_row = gb[:, c:c + 1], gbt[c:c + 1, :]
            beta = gb[:, nh2 + c:nh2 + c + 1]
            tot_row = gbt[2 * nh2 + c:2 * nh2 + c + 1, :]
            decay = jnp.exp(jnp.minimum(g_col - g_row, 0.0))
            a = kk * beta * decay * m_ref[2 * d]
            nt = _neumann_inverse(a, sub_mask, chunk // GDN_SUB)
            eg = jnp.exp(g_col)
            kb = kh * beta
            rhs = jnp.concatenate([vh * beta, kb * eg], axis=1)
            sol = rhs + _dot(nt, rhs)
            u_ref[d, 0, :, sl] = sol[:, :GDN_DK]
            w_ref[d, 0, :, sl] = sol[:, GDN_DK:].astype(bf16)
            qd_ref[d, 0, :, sl] = (qh * eg).astype(bf16)
            qkm = qk * decay * m_ref[2 * d + 1]
            for ci in range(tb // chunk):
                rs = slice(ci * chunk, (ci + 1) * chunk)
                qk_ref[d, 0, rs, h * chunk:(h + 1) * chunk] = qkm[rs, rs].astype(bf16)
            kt_ref[d, 0, sl, :] = (kht * jnp.exp(tot_row - g_row)).astype(bf16)


def _gdn_intra(q, k, v, gb):
    b, l, w = q.shape
    tb = min(GDN_BLOCK, l)
    chunk = min(GDN_CHUNK, tb)
    same, i, j = _chunk_masks(tb, chunk)
    sub = (i // GDN_SUB) == (j // GDN_SUB)
    masks = jnp.asarray(np.stack([same & (i > j), same & (i >= j), same & (i < j), same & (i <= j), sub]), f32)
    tok = lambda width: pl.BlockSpec((1, tb, width), lambda bi, ji: (bi, ji, 0))
    dtok = lambda width: pl.BlockSpec((2, 1, tb, width), lambda bi, ji: (0, bi, ji, 0))
    sds = lambda width, dt: jax.ShapeDtypeStruct((2, b, l, width), dt)
    return pl.pallas_call(
        _gdn_intra_kernel,
        grid=(b, l // tb),
        in_specs=[tok(w), tok(w), tok(w), tok(LANES),
                  pl.BlockSpec((5, tb, tb), lambda bi, ji: (0, 0, 0))],
        out_specs=[dtok(w), dtok(w), dtok(w), dtok(GDN_HEADS * chunk),
                   pl.BlockSpec((2, 1, w, tb), lambda bi, ji: (0, bi, 0, ji))],
        out_shape=[sds(w, f32), sds(w, bf16), sds(w, bf16), sds(GDN_HEADS * chunk, bf16),
                   jax.ShapeDtypeStruct((2, b, w, l), bf16)],
        compiler_params=_params("parallel", "parallel"),
        name="gdn_intra",
    )(q, k, v, gb, masks)


def _gdn_recur_kernel(s0_ref, uf, wf, qf, kf, tf, gf, ub, wb, qb, kb, tb_, gbk, of_ref, ob_ref, sfin_ref, s_ref):
    n = pl.program_id(0)
    nb = uf.shape[1]
    chunk = uf.shape[2]

    @pl.when(n == 0)
    def _():
        s_ref[...] = s0_ref[...]

    dirs = ((uf, wf, qf, kf, tf, gf, of_ref), (ub, wb, qb, kb, tb_, gbk, ob_ref))
    for d, (u_r, w_r, q_r, qk_r, kt_r, g_r, o_r) in enumerate(dirs):
        for b in range(nb):
            for h in range(GDN_HEADS):
                idx = (d * nb + b) * GDN_HEADS + h
                sl = slice(h * GDN_DK, (h + 1) * GDN_DK)
                c = 4 * GDN_HEADS + d * GDN_HEADS + h
                s = s_ref[idx]
                sb = s.astype(bf16)
                v_new = u_r[0, b, :, sl] - jnp.dot(w_r[0, b, :, sl], sb, preferred_element_type=f32)
                vb = v_new.astype(bf16)
                o = jnp.dot(q_r[0, b, :, sl], sb, preferred_element_type=f32)
                o = o + jnp.dot(qk_r[0, b, :, h * chunk:(h + 1) * chunk], vb, preferred_element_type=f32)
                o_r[b, :, sl] = o
                dec = jnp.exp(g_r[b, 0:1, c:c + 1])
                s_ref[idx] = s * dec + jnp.dot(kt_r[0, b, sl, :], vb, preferred_element_type=f32)

    @pl.when(n == pl.num_programs(0) - 1)
    def _():
        sfin_ref[...] = s_ref[...]


def _gdn_recur(s0, u, w, qd, qk, kt, gb):
    _, b, l, wd = u.shape
    chunk = min(GDN_CHUNK, l)
    nc = l // chunk
    fwd = lambda n: n
    bwd = lambda n: nc - 1 - n

    def specs(order, d):
        tok = lambda width: pl.BlockSpec((1, b, chunk, width), lambda n: (d, 0, order(n), 0))
        return [tok(wd), tok(wd), tok(wd), tok(GDN_HEADS * chunk),
                pl.BlockSpec((1, b, wd, chunk), lambda n: (d, 0, 0, order(n))),
                pl.BlockSpec((b, chunk, LANES), lambda n: (0, order(n), 0))]

    nstate = 2 * b * GDN_HEADS
    state = pl.BlockSpec((nstate, GDN_DK, GDN_DK), lambda n: (0, 0, 0))
    out_tok = lambda order: pl.BlockSpec((b, chunk, wd), lambda n: (0, order(n), 0))
    args = (u, w, qd, qk, kt, gb)
    return pl.pallas_call(
        _gdn_recur_kernel,
        grid=(nc,),
        in_specs=[state] + specs(fwd, 0) + specs(bwd, 1),
        out_specs=[out_tok(fwd), out_tok(bwd), state],
        out_shape=[jax.ShapeDtypeStruct((b, l, wd), f32)] * 2 + [jax.ShapeDtypeStruct((nstate, GDN_DK, GDN_DK), f32)],
        scratch_shapes=[pltpu.VMEM((nstate, GDN_DK, GDN_DK), f32)],
        compiler_params=_params("arbitrary"),
        name="gdn_recur",
    )(s0, *args, *args)


def _dft_mats(n):
    k = np.arange(n)
    ang = 2.0 * np.pi * ((k[:, None] * k[None, :]) % n) / n
    return np.cos(ang), np.sin(ang)


def _channel_dft():
    c, s = _dft_mats(FNET_GDIM)
    eye = np.eye(FNET_GROUPS)
    return jnp.asarray(np.kron(eye, c), f32), jnp.asarray(np.kron(eye, s), f32)


def _fft1_kernel(x_ref, c_ref, s_ref, tc_ref, ts_ref, br_ref, bi_ref):
    x = x_ref[0]
    ar = _dot_f32(c_ref[...], x)
    ai = -_dot_f32(s_ref[...], x)
    tc, ts = tc_ref[...], ts_ref[...]
    br = ar * tc + ai * ts
    bi = ai * tc - ar * ts
    for t in range(br_ref.shape[1]):
        br_ref[0, t] = br[:, t * FNET_W:(t + 1) * FNET_W]
        bi_ref[0, t] = bi[:, t * FNET_W:(t + 1) * FNET_W]


def _fft2_kernel(br_ref, bi_ref, c_ref, s_ref, cc_ref, sc_ref, o_ref, *, scale):
    br, bi = br_ref[0], bi_ref[0]
    c, s = c_ref[...], s_ref[...]
    xr = _dot_f32(c, br) + _dot_f32(s, bi)
    xi = _dot_f32(c, bi) - _dot_f32(s, br)
    for t in range(xr.shape[1] // FNET_W):
        sl = slice(t * FNET_W, (t + 1) * FNET_W)
        o_ref[0, :, sl] = (_dot_f32(xr[:, sl], cc_ref[...]) + _dot_f32(xi[:, sl], sc_ref[...])) * scale


def _fnet_long(f, n1, n2):
    b, l, w = f.shape
    tn = FNET_LANE_TILE
    npos = tn // w
    c1, s1 = _dft_mats(n1)
    c2, s2 = _dft_mats(n2)
    ang = 2.0 * np.pi * ((np.arange(n1)[:, None] * np.arange(n2)[None, :]) % l) / l
    tc = jnp.asarray(np.repeat(np.cos(ang), w, axis=1), f32)
    ts = jnp.asarray(np.repeat(np.sin(ang), w, axis=1), f32)
    cc, sc = _channel_dft()
    const = lambda shape: pl.BlockSpec(shape, lambda bi, ji: (0,) * len(shape))
    br, bi = pl.pallas_call(
        _fft1_kernel,
        grid=(b, n2 * w // tn),
        in_specs=[pl.BlockSpec((1, n1, tn), lambda bi, ji: (bi, 0, ji)),
                  const((n1, n1)), const((n1, n1)),
                  pl.BlockSpec((n1, tn), lambda bi, ji: (0, ji)),
                  pl.BlockSpec((n1, tn), lambda bi, ji: (0, ji))],
        out_specs=[pl.BlockSpec((1, npos, n1, w), lambda bi, ji: (bi, ji, 0, 0))] * 2,
        out_shape=[jax.ShapeDtypeStruct((b, n2, n1, w), f32)] * 2,
        compiler_params=_params("parallel", "parallel"),
        name="fnet_stage1",
    )(f.reshape(b, n1, n2 * w), jnp.asarray(c1, f32), jnp.asarray(s1, f32), tc, ts)
    out = pl.pallas_call(
        functools.partial(_fft2_kernel, scale=1.0 / math.sqrt(l * FNET_GDIM)),
        grid=(b, n1 * w // tn),
        in_specs=[pl.BlockSpec((1, n2, tn), lambda bi, ji: (bi, 0, ji))] * 2
                 + [const((n2, n2)), const((n2, n2)), const((w, w)), const((w, w))],
        out_specs=pl.BlockSpec((1, n2, tn), lambda bi, ji: (bi, 0, ji)),
        out_shape=jax.ShapeDtypeStruct((b, n2, n1 * w), f32),
        compiler_params=_params("parallel", "parallel"),
        name="fnet_stage2",
    )(br.reshape(b, n2, n1 * w), bi.reshape(b, n2, n1 * w),
      jnp.asarray(c2, f32), jnp.asarray(s2, f32), cc, sc)
    return out.reshape(b, l, w)


def _fft_short_kernel(x_ref, c_ref, s_ref, cc_ref, sc_ref, o_ref, *, scale):
    x = x_ref[0]
    zr = _dot_f32(c_ref[...], x)
    zi = -_dot_f32(s_ref[...], x)
    o_ref[0] = (_dot_f32(zr, cc_ref[...]) + _dot_f32(zi, sc_ref[...])) * scale


def _fnet_short(f):
    b, l, w = f.shape
    c, s = _dft_mats(l)
    cc, sc = _channel_dft()
    const = lambda shape: pl.BlockSpec(shape, lambda bi: (0,) * len(shape))
    return pl.pallas_call(
        functools.partial(_fft_short_kernel, scale=1.0 / math.sqrt(l * FNET_GDIM)),
        grid=(b,),
        in_specs=[pl.BlockSpec((1, l, w), lambda bi: (bi, 0, 0)),
                  const((l, l)), const((l, l)), const((w, w)), const((w, w))],
        out_specs=pl.BlockSpec((1, l, w), lambda bi: (bi, 0, 0)),
        out_shape=jax.ShapeDtypeStruct((b, l, w), f32),
        compiler_params=_params("parallel"),
        name="fnet_short",
    )(f, jnp.asarray(c, f32), jnp.asarray(s, f32), cc, sc)


def _fnet(f):
    l = f.shape[1]
    n1 = 1 << (int(math.log2(l)) // 2)
    n2 = l // n1
    if l <= FNET_SHORT_MAX or n2 * FNET_W % FNET_LANE_TILE or n1 * FNET_W % FNET_LANE_TILE:
        return _fnet_short(f)
    return _fnet_long(f, n1, n2)


CONV_HALO = 16


def _conv_kernel(gl_ref, dw_ref, db_ref, gw_ref, gb_ref, avg_ref, o_ref, pad_ref, *, row_len):
    tl = gl_ref.shape[1]
    nr = tl // row_len
    y = gl_ref[0, :, :CONV_W] * jax.nn.sigmoid(gl_ref[0, :, CONV_W:])
    zeros = jnp.zeros((nr, CONV_HALO, CONV_W), f32)
    pad_ref[:, 0:CONV_HALO, :] = zeros
    pad_ref[:, CONV_HALO + row_len:, :] = zeros
    pad_ref[:, CONV_HALO:CONV_HALO + row_len, :] = y.reshape(nr, row_len, CONV_W)
    first = CONV_HALO - CONV_K // 2
    acc = jnp.zeros((nr, row_len, CONV_W), f32)
    for k in range(CONV_K):
        acc = acc + dw_ref[k:k + 1, :].reshape(1, 1, CONV_W) * pad_ref[:, first + k:first + k + row_len, :]
    yc = acc.reshape(tl, CONV_W) + db_ref[...]
    mu = _dot_f32(yc, avg_ref[...])
    cen = yc - mu
    var = _dot_f32(cen * cen, avg_ref[...])
    yn = cen * lax.rsqrt(var + EPS) * gw_ref[...] + gb_ref[...]
    o_ref[0] = _silu(yn)


def _conv_module(gl, dw_w, dw_b, gn_w, gn_b, row_len):
    b, l, w2 = gl.shape
    tl = max(row_len, min(CONV_TILE, l))
    gd = CONV_W // CONV_GROUPS
    avg = jnp.asarray(np.kron(np.eye(CONV_GROUPS), np.full((gd, gd), 1.0 / gd)), f32)
    const = lambda shape: pl.BlockSpec(shape, lambda bi, ji: (0,) * len(shape))
    vec = lambda v: v.reshape(1, CONV_W)
    return pl.pallas_call(
        functools.partial(_conv_kernel, row_len=row_len),
        grid=(b, l // tl),
        in_specs=[pl.BlockSpec((1, tl, w2), lambda bi, ji: (bi, ji, 0)),
                  const((CONV_K, CONV_W)), const((1, CONV_W)), const((1, CONV_W)), const((1, CONV_W)),
                  const((CONV_W, CONV_W))],
        out_specs=pl.BlockSpec((1, tl, CONV_W), lambda bi, ji: (bi, ji, 0)),
        out_shape=jax.ShapeDtypeStruct((b, l, CONV_W), f32),
        scratch_shapes=[pltpu.VMEM((tl // row_len, row_len + 2 * CONV_HALO, CONV_W), f32)],
        compiler_params=_params("parallel", "parallel"),
        name="conv_module",
    )(gl, dw_w, vec(dw_b), vec(gn_w), vec(gn_b), avg)


def _mix_out_kernel(of_ref, ob_ref, z_ref, yf_ref, yc_ref, x_ref, gt_ref, gnw_ref, w_ref, o_ref):
    acc = jnp.dot(yf_ref[0].astype(bf16), w_ref[GDN_W:GDN_W + FNET_W, :], preferred_element_type=f32)
    acc = acc + jnp.dot(yc_ref[0].astype(bf16), w_ref[GDN_W + FNET_W:, :], preferred_element_type=f32)
    for h in range(GDN_HEADS):
        sl = slice(h * GDN_DK, (h + 1) * GDN_DK)
        o = of_ref[0, :, sl] + ob_ref[0, :, sl]
        o = o * lax.rsqrt(jnp.mean(o * o, axis=-1, keepdims=True) + EPS) * gnw_ref[...]
        y = (o * _silu(z_ref[0, :, sl])).astype(bf16)
        acc = acc + jnp.dot(y, w_ref[sl, :], preferred_element_type=f32)
    o_ref[0] = x_ref[0] + gt_ref[0] * acc


def _mix_out(o_f, o_b, z, y_f, y_c, x, gate, gdn_norm, w_out_b):
    b, l, d = x.shape
    tm = min(ROW_TILE, l)
    tok = lambda width: pl.BlockSpec((1, tm, width), lambda i, j: (i, j, 0))
    return pl.pallas_call(
        _mix_out_kernel,
        grid=(b, l // tm),
        in_specs=[tok(GDN_W), tok(GDN_W), tok(GDN_W), tok(FNET_W), tok(CONV_W), tok(d),
                  pl.BlockSpec((1, 1, d), lambda i, j: (i, 0, 0)),
                  pl.BlockSpec((1, GDN_DK), lambda i, j: (0, 0)),
                  pl.BlockSpec((d, d), lambda i, j: (0, 0))],
        out_specs=tok(d),
        out_shape=jax.ShapeDtypeStruct((b, l, d), f32),
        compiler_params=_params("parallel", "parallel"),
        name="mix_out_proj",
    )(o_f, o_b, z, y_f, y_c, x, gate, gdn_norm.reshape(1, GDN_DK), w_out_b)


def _peer_query_kernel(x_ref, nw_ref, sc_ref, sh_ref, wq_ref, keys_ref, h_ref, s_ref):
    hb = _ada_norm(x_ref[0], nw_ref[...], sc_ref[0], sh_ref[0]).astype(bf16)
    h_ref[0] = hb
    for hp in range(2 * PEER_HEADS):
        sl = slice(hp * PEER_DH, (hp + 1) * PEER_DH)
        q = jnp.dot(hb, wq_ref[:, sl], preferred_element_type=f32)
        s_ref[0, hp] = _dot_nt(keys_ref[hp], q)


def _peer_query(x, nw, sc, sh, wq_b, keys_b):
    b, l, d = x.shape
    tm = min(ROW_TILE, l)
    nq = wq_b.shape[1]
    nhp = 2 * PEER_HEADS
    vec = pl.BlockSpec((1, 1, d), lambda i, j: (i, 0, 0))
    return pl.pallas_call(
        _peer_query_kernel,
        grid=(b, l // tm),
        in_specs=[pl.BlockSpec((1, tm, d), lambda i, j: (i, j, 0)),
                  pl.BlockSpec((1, d), lambda i, j: (0, 0)), vec, vec,
                  pl.BlockSpec((d, nq), lambda i, j: (0, 0)),
                  pl.BlockSpec((nhp, PEER_NKEYS, PEER_DH), lambda i, j: (0, 0, 0))],
        out_specs=[pl.BlockSpec((1, tm, d), lambda i, j: (i, j, 0)),
                   pl.BlockSpec((1, nhp, PEER_NKEYS, tm), lambda i, j: (i, 0, 0, j))],
        out_shape=[jax.ShapeDtypeStruct((b, l, d), bf16),
                   jax.ShapeDtypeStruct((b, nhp, PEER_NKEYS, l), f32)],
        compiler_params=_params("parallel", "parallel"),
        name="peer_query",
    )(x, nw.reshape(1, d), sc, sh, wq_b, keys_b)


CAND_SHORT = 8
CAND_ROWS = PEER_TOPK + (PEER_TOPK - 1) * CAND_SHORT


def _top_values(cur, out_ref):
    for r in range(PEER_TOPK):
        m = jnp.max(cur, axis=0, keepdims=True)
        out_ref[r:r + 1, :] = m
        if r + 1 < PEER_TOPK:
            cur = jnp.where(cur == m, NEG_INF, cur)
    return m


def _fill_candidates(av, bv, cand_ref):
    cand_ref[0:PEER_TOPK, :] = av[0:1, :] * bv
    for r in range(1, PEER_TOPK):
        lo = PEER_TOPK + (r - 1) * CAND_SHORT
        cand_ref[lo:lo + CAND_SHORT, :] = av[r:r + 1, :] * bv[0:CAND_SHORT, :]


def _peer_topk_kernel(s_ref, a_ref, b_ref, th_ref, sv1_ref, sv2_ref, cand_ref, top_ref):
    def head(h, carry):
        s1 = s_ref[0, 2 * h]
        s2 = s_ref[0, 2 * h + 1]
        thr1 = _top_values(s1, sv1_ref)
        thr2 = _top_values(s2, sv2_ref)
        m1, m2 = sv1_ref[0:1, :], sv2_ref[0:1, :]
        av = jnp.exp(sv1_ref[...] - m1)
        bv = jnp.exp(sv2_ref[...] - m2)
        _fill_candidates(av, bv, cand_ref)
        theta = _top_values(cand_ref[...], top_ref)
        cand = cand_ref[...]
        sel = cand >= theta
        zsum = jnp.sum(jnp.where(sel, cand, 0.0), axis=0, keepdims=True)
        rz = 1.0 / zsum
        _fill_candidates(av * rz, bv, cand_ref)
        th_ref[0, h] = jnp.min(jnp.where(sel, cand_ref[...], jnp.inf), axis=0, keepdims=True)
        a_ref[0, h] = jnp.where(s1 >= thr1, jnp.exp(s1 - m1) * rz, 0.0)
        b_ref[0, h] = jnp.where(s2 >= thr2, jnp.exp(s2 - m2), 0.0)
        return carry

    lax.fori_loop(0, PEER_HEADS, head, 0)


def _peer_topk(scores):
    b, nhp, nk, l = scores.shape
    tt = min(PEER_TOPK_TILE, l)
    ab_spec = pl.BlockSpec((1, PEER_HEADS, nk, tt), lambda i, j: (i, 0, 0, j))
    return pl.pallas_call(
        _peer_topk_kernel,
        grid=(b, l // tt),
        in_specs=[pl.BlockSpec((1, nhp, nk, tt), lambda i, j: (i, 0, 0, j))],
        out_specs=[ab_spec, ab_spec, pl.BlockSpec((1, PEER_HEADS, 1, tt), lambda i, j: (i, 0, 0, j))],
        out_shape=[jax.ShapeDtypeStruct((b, PEER_HEADS, nk, l), f32)] * 2
                  + [jax.ShapeDtypeStruct((b, PEER_HEADS, 1, l), f32)],
        scratch_shapes=[pltpu.VMEM((PEER_TOPK, tt), f32), pltpu.VMEM((PEER_TOPK, tt), f32),
                        pltpu.VMEM((CAND_ROWS, tt), f32), pltpu.VMEM((PEER_TOPK, tt), f32)],
        compiler_params=_params("parallel", "parallel"),
        name="peer_topk",
    )(scores)


def _gelu_tanh(x):
    return 0.5 * x * (1.0 + jnp.tanh(math.sqrt(2.0 / math.pi) * (x + 0.044715 * (x * x * x))))


def _peer_dense_kernel(h_ref, a_ref, b_ref, th_ref, u_ref, vt_ref, x_ref, gt_ref, o_ref,
                       acc_ref, s_ref, g_ref):
    e = pl.program_id(2)
    tt = h_ref.shape[1]
    rows_per_key = PEER_NKEYS
    n_first = u_ref.shape[0] // rows_per_key

    @pl.when(e == 0)
    def _():
        acc_ref[...] = jnp.zeros_like(acc_ref)

    s_ref[...] = lax.dot_general(u_ref[...], h_ref[0], (((1,), (1,)), ((), ())), preferred_element_type=f32)
    for lc in range(tt // PEER_LANE_CHUNK):
        lanes = slice(lc * PEER_LANE_CHUNK, (lc + 1) * PEER_LANE_CHUNK)

        def first_key(ii, carry):
            rows = pl.ds(pl.multiple_of(ii * rows_per_key, rows_per_key), rows_per_key)
            wsum = jnp.zeros((rows_per_key, PEER_LANE_CHUNK), f32)
            for h in range(PEER_HEADS):
                p = a_ref[0, h, pl.ds(ii, 1), lanes] * b_ref[0, h, :, lanes]
                wsum = wsum + jnp.where(p >= th_ref[0, h, :, lanes], p, 0.0)
            g_ref[rows, lanes] = (_gelu_tanh(s_ref[rows, lanes]) * wsum).astype(bf16)
            return carry

        lax.fori_loop(0, n_first, first_key, 0)
    acc_ref[...] += jnp.dot(vt_ref[...], g_ref[...], preferred_element_type=f32)

    @pl.when(e == pl.num_programs(2) - 1)
    def _():
        o_ref[0] = x_ref[0] + gt_ref[0] * acc_ref[...].T


def _peer_dense(hb, a, bsel, theta, u_b, vt_b, x, gate):
    b, l, d = x.shape
    ne = u_b.shape[0]
    tt = min(PEER_TOKEN_TILE, l)
    te = PEER_EXPERT_TILE
    nfirst = te // PEER_NKEYS
    return pl.pallas_call(
        _peer_dense_kernel,
        grid=(b, l // tt, ne // te),
        in_specs=[pl.BlockSpec((1, tt, d), lambda i, j, e: (i, j, 0)),
                  pl.BlockSpec((1, PEER_HEADS, nfirst, tt), lambda i, j, e: (i, 0, e, j)),
                  pl.BlockSpec((1, PEER_HEADS, PEER_NKEYS, tt), lambda i, j, e: (i, 0, 0, j)),
                  pl.BlockSpec((1, PEER_HEADS, 1, tt), lambda i, j, e: (i, 0, 0, j)),
                  pl.BlockSpec((te, d), lambda i, j, e: (e, 0)),
                  pl.BlockSpec((d, te), lambda i, j, e: (0, e)),
                  pl.BlockSpec((1, tt, d), lambda i, j, e: (i, j, 0)),
                  pl.BlockSpec((1, 1, d), lambda i, j, e: (i, 0, 0))],
        out_specs=pl.BlockSpec((1, tt, d), lambda i, j, e: (i, j, 0)),
        out_shape=jax.ShapeDtypeStruct((b, l, d), f32),
        scratch_shapes=[pltpu.VMEM((d, tt), f32), pltpu.VMEM((te, tt), f32), pltpu.VMEM((te, tt), bf16)],
        compiler_params=_params("parallel", "parallel", "arbitrary"),
        name="peer_dense",
    )(hb, a, bsel, theta, u_b, vt_b, x, gate)


def _final_norm_kernel(x_ref, w_ref, o_ref):
    x = x_ref[0]
    o_ref[0] = x * lax.rsqrt(jnp.mean(x * x, axis=-1, keepdims=True) + EPS) * w_ref[...]


def _final_norm(x, w):
    b, l, d = x.shape
    tm = min(2 * ROW_TILE, l)
    return pl.pallas_call(
        _final_norm_kernel,
        grid=(b, l // tm),
        in_specs=[pl.BlockSpec((1, tm, d), lambda i, j: (i, j, 0)), pl.BlockSpec((1, d), lambda i, j: (0, 0))],
        out_specs=pl.BlockSpec((1, tm, d), lambda i, j: (i, j, 0)),
        out_shape=jax.ShapeDtypeStruct((b, l, d), f32),
        compiler_params=_params("parallel", "parallel"),
        name="final_norm",
    )(x, w.reshape(1, d))


def _gdn(qkv, ab, s0, lw):
    q, k, v, gb = _gdn_prep(qkv, ab, lw["conv_qkv"], lw["a_log"], lw["dt_bias"])
    return _gdn_recur(s0, *_gdn_intra(q, k, v, gb), gb)


def _mixer(x, proj, o_f, o_b, gate, row_len, lw):
    _, z, f, cgl, _ = proj
    y_f = _fnet(f)
    y_c = _conv_module(cgl, lw["dw_w"], lw["dw_b"], lw["gn_w"], lw["gn_b"], row_len)
    return _mix_out(o_f, o_b, z, y_f, y_c, x, gate, lw["gdn_norm"], lw["w_out"])


def _peer(x, sc, sh, gate, lw):
    hb, scores = _peer_query(x, lw["norm2"], sc, sh, lw["peer_wq"], lw["peer_keys"])
    a, bsel, theta = _peer_topk(scores)
    return _peer_dense(hb, a, bsel, theta, lw["peer_u"], lw["peer_vt"], x, gate)


def kernel(x, c, ctx, c_ctx, w_mod, b_mod, norm1, norm2, w_in, conv_qkv, a_log, dt_bias, gdn_norm, dw_w, dw_b,
           gn_w, gn_b, w_out, peer_wq, peer_keys, peer_u, peer_v, final_norm):
    bsz, _, d = x.shape
    depth = w_mod.shape[0]
    cc = jnp.zeros((SUBLANES, d), f32).at[:bsz].set(c).at[bsz].set(c_ctx)
    mods = _modulation(cc, w_mod, b_mod)
    xc = ctx
    for i in range(depth):
        last = i == depth - 1
        lw = {
            "norm2": norm2[i], "conv_qkv": conv_qkv[i], "a_log": a_log[i], "dt_bias": dt_bias[i],
            "gdn_norm": gdn_norm[i], "dw_w": dw_w[i], "dw_b": dw_b[i], "gn_w": gn_w[i], "gn_b": gn_b[i],
            "w_out": w_out[i].astype(bf16), "peer_wq": peer_wq[i].astype(bf16),
            "peer_keys": peer_keys[i].reshape(2 * PEER_HEADS, PEER_NKEYS, PEER_DH).astype(bf16),
            "peer_u": peer_u[i].astype(bf16), "peer_vt": peer_v[i].T.astype(bf16),
        }
        w_in_p = _pack_w_in(w_in[i])
        mod = mods[i, :bsz].reshape(bsz, 6, 1, d)
        mod_c = jnp.broadcast_to(mods[i, bsz].reshape(1, 6, 1, d), (bsz, 6, 1, d))
        sh1, sc1, gt1, sh2, sc2, gt2 = (mod[:, j] for j in range(6))
        csh1, csc1, cgt1, csh2, csc2, cgt2 = (mod_c[:, j] for j in range(6))

        pc = _norm_in(xc, norm1[i], csc1, csh1, w_in_p)
        p = _norm_in(x, norm1[i], sc1, sh1, w_in_p)
        zero = jnp.zeros((2 * bsz * GDN_HEADS, GDN_DK, GDN_DK), f32)
        oc_f, oc_b, s_ctx = _gdn(pc[0], pc[4], zero, lw)
        o_f, o_b, _ = _gdn(p[0], p[4], s_ctx, lw)
        x = _mixer(x, p, o_f, o_b, gt1, GRID_W, lw)
        x = _peer(x, sc2, sh2, gt2, lw)
        if not last:
            xc = _mixer(xc, pc, oc_f, oc_b, cgt1, xc.shape[1], lw)
            xc = _peer(xc, csc2, csh2, cgt2, lw)
    return _final_norm(x, final_norm)
```

```python
import functools
import math

import numpy as np
import jax
import jax.numpy as jnp
from jax import lax
from jax.experimental import pallas as pl
from jax.experimental.pallas import tpu as pltpu

f32 = jnp.float32
bf16 = jnp.bfloat16

D_MODEL = 1024
DEPTH = 2
GRID_W = 64
GDN_DK = 128
GDN_HEADS = 4
GDN_W = 512
FNET_W = 256
FNET_GROUPS = 4
FNET_GDIM = 64
CONV_W = 256
CONV_GROUPS = 4
CONV_K = 31
QKV_OFF, Z_OFF, A_OFF, B_OFF, F_OFF, C_OFF, IN_W = 0, 1536, 2048, 2056, 2064, 2320, 2832
PEER_HEADS = 8
PEER_NKEYS = 128
PEER_TOPK = 16
PEER_DH = 128
EPS = 1e-6

LANES = 128
SUBLANES = 8
VMEM_LIMIT_BYTES = 56 * 2**20

GDN_CHUNK = 128
GDN_BLOCK = 256
GDN_SUB = 16
ROW_TILE = 256
CONV_TILE = 512
FNET_LANE_TILE = 8 * FNET_W
FNET_SHORT_MAX = 512
PEER_QUERY_TILE = 512
PEER_TOKEN_TILE = 512
PEER_EXPERT_TILE = 1024
PEER_SUB_TILE = 256
PEER_TOPK_TILE = 256
PEER_LANE_CHUNK = 256
NEG_INF = float("-inf")


def _params(*sem):
    return pltpu.CompilerParams(dimension_semantics=sem, vmem_limit_bytes=VMEM_LIMIT_BYTES)


def _dot(a, b):
    return jnp.dot(a.astype(bf16), b.astype(bf16), preferred_element_type=f32)


def _dot_nt(a, b):
    return lax.dot_general(a.astype(bf16), b.astype(bf16), (((1,), (1,)), ((), ())),
                           preferred_element_type=f32)


def _dot_f32(a, b):
    return jnp.dot(a, b, preferred_element_type=f32, precision=lax.Precision.HIGHEST)


def _silu(x):
    return x * jax.nn.sigmoid(x)


def _mod_kernel(c_ref, w_ref, b_ref, o_ref):
    o_ref[0] = _dot(_silu(c_ref[...]), w_ref[0]) + b_ref[0]


def _modulation(cc, w_mod, b_mod):
    depth, d, n = w_mod.shape
    tn = n // 4
    return pl.pallas_call(
        _mod_kernel,
        grid=(depth, n // tn),
        in_specs=[pl.BlockSpec((SUBLANES, d), lambda i, j: (0, 0)),
                  pl.BlockSpec((1, d, tn), lambda i, j: (i, 0, j)),
                  pl.BlockSpec((1, 1, tn), lambda i, j: (i, 0, j))],
        out_specs=pl.BlockSpec((1, SUBLANES, tn), lambda i, j: (i, 0, j)),
        out_shape=jax.ShapeDtypeStruct((depth, SUBLANES, n), f32),
        compiler_params=_params("parallel", "parallel"),
        name="modulation",
    )(cc, w_mod, b_mod.reshape(depth, 1, n))


def _ada_norm(x, nw, sc, sh):
    r = lax.rsqrt(jnp.mean(x * x, axis=-1, keepdims=True) + EPS)
    return (x * r * nw) * (1.0 + sc) + sh


IN_SPLITS = (3 * GDN_W, GDN_W, FNET_W, 2 * CONV_W, LANES)


def _norm_in_kernel(x_ref, nw_ref, sc_ref, sh_ref, w_ref, *out_refs):
    hb = _ada_norm(x_ref[0], nw_ref[...], sc_ref[0], sh_ref[0]).astype(bf16)
    off = 0
    for o_ref, width in zip(out_refs, IN_SPLITS):
        o_ref[0] = jnp.dot(hb, w_ref[:, off:off + width], preferred_element_type=f32)
        off += width


def _pack_w_in(w_in):
    ab = jnp.pad(w_in[:, A_OFF:F_OFF], ((0, 0), (0, LANES - (F_OFF - A_OFF))))
    return jnp.concatenate([w_in[:, QKV_OFF:A_OFF], w_in[:, F_OFF:IN_W], ab], axis=1).astype(bf16)


def _norm_in(x, nw, sc, sh, w_packed):
    b, l, d = x.shape
    tm = min(ROW_TILE, l)
    n = w_packed.shape[1]
    vec = pl.BlockSpec((1, 1, d), lambda i, j: (i, 0, 0))
    return pl.pallas_call(
        _norm_in_kernel,
        grid=(b, l // tm),
        in_specs=[pl.BlockSpec((1, tm, d), lambda i, j: (i, j, 0)),
                  pl.BlockSpec((1, d), lambda i, j: (0, 0)),
                  vec, vec,
                  pl.BlockSpec((d, n), lambda i, j: (0, 0))],
        out_specs=[pl.BlockSpec((1, tm, w), lambda i, j: (i, j, 0)) for w in IN_SPLITS],
        out_shape=[jax.ShapeDtypeStruct((b, l, w), f32) for w in IN_SPLITS],
        compiler_params=_params("parallel", "parallel"),
        name="norm_in_proj",
    )(x, nw.reshape(1, d), sc, sh, w_packed)


def _split3(x):
    x1 = x.astype(bf16)
    r1 = x - x1.astype(f32)
    x2 = r1.astype(bf16)
    x3 = (r1 - x2.astype(f32)).astype(bf16)
    return x1, x2, x3


def _dot01(m_ref, parts):
    m = m_ref[...]
    return sum(jnp.dot(m, p, preferred_element_type=f32) for p in parts)


def _gdn_prep_kernel(prev_ref, x_ref, next_ref, ab_ref, cw_ref, alog_ref, dtb_ref,
                     tril_ref, triu_ref, same_ref, q_ref, k_ref, v_ref, gb_ref):
    j = pl.program_id(1)
    tl = x_ref.shape[1]
    first = j == 0
    last = j == pl.num_programs(1) - 1
    row = lax.broadcasted_iota(jnp.int32, (tl, LANES), 0)
    outs = (q_ref, k_ref, v_ref)
    for part in range(3):
        for h in range(GDN_HEADS):
            lo = part * GDN_W + h * GDN_DK
            sl = slice(lo, lo + GDN_DK)
            x = x_ref[0, :, sl]
            prev_row = jnp.where(first, 0.0, prev_ref[0, SUBLANES - 1:SUBLANES, sl])
            next_row = jnp.where(last, 0.0, next_ref[0, 0:1, sl])
            xp = jnp.where(row == 0, prev_row, pltpu.roll(x, 1, axis=0))
            xn = jnp.where(row == tl - 1, next_row, pltpu.roll(x, tl - 1, axis=0))
            y = _silu(cw_ref[0:1, sl] * xp + cw_ref[1:2, sl] * x + cw_ref[2:3, sl] * xn)
            if part < 2:
                y = y * lax.rsqrt(jnp.sum(y * y, axis=-1, keepdims=True) + EPS)
            if part == 0:
                y = y * (GDN_DK ** -0.5)
            outs[part][0, :, h * GDN_DK:(h + 1) * GDN_DK] = y
    ab = ab_ref[0]
    col = lax.broadcasted_iota(jnp.int32, ab.shape, 1)
    nh2 = 2 * GDN_HEADS
    sp = jnp.maximum(ab + dtb_ref[...], 0.0) + jnp.log1p(jnp.exp(-jnp.abs(ab + dtb_ref[...])))
    g = jnp.where(col < nh2, -jnp.exp(alog_ref[...]) * sp, 0.0)
    parts = _split3(g)
    g_f = _dot01(tril_ref, parts)
    g_b = _dot01(triu_ref, parts)
    tot = _dot01(same_ref, parts)
    beta = jax.nn.sigmoid(ab)
    out = jnp.where(col < GDN_HEADS, g_f, jnp.where(col < nh2, g_b, 0.0))
    out = out + jnp.where((col >= nh2) & (col < 2 * nh2), beta, 0.0)
    out = out + pltpu.roll(tot, 2 * nh2, axis=1)
    gb_ref[0] = out


def _chunk_masks(tb, chunk):
    i = np.arange(tb)[:, None]
    j = np.arange(tb)[None, :]
    same = (i // chunk) == (j // chunk)
    return same, i, j


def _gdn_prep(qkv, ab, conv_qkv, a_log, dt_bias):
    b, l, w = qkv.shape
    tl = min(GDN_BLOCK, l)
    nblk = tl // SUBLANES
    same, i, j = _chunk_masks(tl, GDN_CHUNK)
    tril = jnp.asarray(same & (i >= j), bf16)
    triu = jnp.asarray(same & (i <= j), bf16)
    samem = jnp.asarray(same, bf16)
    row = lambda v: jnp.pad(v.reshape(1, -1).astype(f32), ((0, 0), (0, LANES - v.size)))
    nlast = l // SUBLANES - 1
    const = lambda shape: pl.BlockSpec(shape, lambda bi, ji: (0,) * len(shape))
    tok = lambda width: pl.BlockSpec((1, tl, width), lambda bi, ji: (bi, ji, 0))
    return pl.pallas_call(
        _gdn_prep_kernel,
        grid=(b, l // tl),
        in_specs=[pl.BlockSpec((1, SUBLANES, w), lambda bi, ji: (bi, jnp.maximum(ji * nblk - 1, 0), 0)),
                  tok(w),
                  pl.BlockSpec((1, SUBLANES, w), lambda bi, ji: (bi, jnp.minimum((ji + 1) * nblk, nlast), 0)),
                  tok(LANES),
                  const((3, w)), const((1, LANES)), const((1, LANES)),
                  const((tl, tl)), const((tl, tl)), const((tl, tl))],
        out_specs=[tok(GDN_W), tok(GDN_W), tok(GDN_W), tok(LANES)],
        out_shape=[jax.ShapeDtypeStruct((b, l, GDN_W), f32)] * 3 + [jax.ShapeDtypeStruct((b, l, LANES), f32)],
        compiler_params=_params("parallel", "parallel"),
        name="gdn_prep",
    )(qkv, qkv, qkv, ab, conv_qkv, row(a_log), row(dt_bias), tril, triu, samem)


def _neumann_inverse(a_list, sub_mask, n_sub):
    d = [a * sub_mask for a in a_list]
    e = [a - x for a, x in zip(a_list, d)]
    n = [-x for x in d]
    p = n
    for _ in range(int(math.log2(GDN_SUB)) - 1):
        p = [_dot(x, x) for x in p]
        n = [nn + pp + _dot(nn, pp) for nn, pp in zip(n, p)]
    q = [-(ee + _dot(nn, ee)) for nn, ee in zip(n, e)]
    y, p = q, q
    for _ in range(int(math.log2(n_sub)) - 1):
        p = [_dot(x, x) for x in p]
        y = [yy + pp + _dot(yy, pp) for yy, pp in zip(y, p)]
    return [yy + nn + _dot(yy, nn) for yy, nn in zip(y, n)]


def _gdn_intra_kernel(q_ref, k_ref, v_ref, gb_ref, m_ref, u_ref, w_ref, qd_ref, qk_ref, kt_ref):
    tb = q_ref.shape[1]
    chunk = min(GDN_CHUNK, tb)
    gb = gb_ref[0]
    gbt = gb.T
    nh2 = 2 * GDN_HEADS
    chains, a_list, rhs_list = [], [], []
    for h in range(GDN_HEADS):
        sl = slice(h * GDN_DK, (h + 1) * GDN_DK)
        qh, kh, vh = q_ref[0, :, sl], k_ref[0, :, sl], v_ref[0, :, sl]
        kk = _dot_nt(kh, kh)
        qk = _dot_nt(qh, kh)
        kht = kh.T
        for d in range(2):
            c = d * GDN_HEADS + h
            g_col, g_row = gb[:, c:c + 1], gbt[c:c + 1, :]
            beta = gb[:, nh2 + c:nh2 + c + 1]
            tot_row = gbt[2 * nh2 + c:2 * nh2 + c + 1, :]
            decay = jnp.exp(jnp.minimum(g_col - g_row, 0.0))
            eg = jnp.exp(g_col)
            kb = kh * beta
            chains.append((d, sl))
            a_list.append(kk * beta * decay * m_ref[2 * d])
            rhs_list.append(jnp.concatenate([vh * beta, kb * eg], axis=1))
            qd_ref[d, 0, :, sl] = (qh * eg).astype(bf16)
            qkm = qk * decay * m_ref[2 * d + 1]
            for ci in range(tb // chunk):
                rs = slice(ci * chunk, (ci + 1) * chunk)
                qk_ref[d, 0, rs, h * chunk:(h + 1) * chunk] = qkm[rs, rs].astype(bf16)
            kt_ref[d, 0, sl, :] = (kht * jnp.exp(tot_row - g_row)).astype(bf16)
    nt_list = _neumann_inverse(a_list, m_ref[4], chunk // GDN_SUB)
    for (d, sl), nt, rhs in zip(chains, nt_list, rhs_list):
        sol = rhs + _dot(nt, rhs)
        u_ref[d, 0, :, sl] = sol[:, :GDN_DK]
        w_ref[d, 0, :, sl] = sol[:, GDN_DK:].astype(bf16)


def _gdn_intra(q, k, v, gb):
    b, l, w = q.shape
    tb = min(GDN_BLOCK, l)
    chunk = min(GDN_CHUNK, tb)
    same, i, j = _chunk_masks(tb, chunk)
    sub = (i // GDN_SUB) == (j // GDN_SUB)
    masks = jnp.asarray(np.stack([same & (i > j), same & (i >= j), same & (i < j), same & (i <= j), sub]), f32)
    tok = lambda width: pl.BlockSpec((1, tb, width), lambda bi, ji: (bi, ji, 0))
    dtok = lambda width: pl.BlockSpec((2, 1, tb, width), lambda bi, ji: (0, bi, ji, 0))
    sds = lambda width, dt: jax.ShapeDtypeStruct((2, b, l, width), dt)
    return pl.pallas_call(
        _gdn_intra_kernel,
        grid=(b, l // tb),
        in_specs=[tok(w), tok(w), tok(w), tok(LANES),
                  pl.BlockSpec((5, tb, tb), lambda bi, ji: (0, 0, 0))],
        out_specs=[dtok(w), dtok(w), dtok(w), dtok(GDN_HEADS * chunk),
                   pl.BlockSpec((2, 1, w, tb), lambda bi, ji: (0, bi, 0, ji))],
        out_shape=[sds(w, f32), sds(w, bf16), sds(w, bf16), sds(GDN_HEADS * chunk, bf16),
                   jax.ShapeDtypeStruct((2, b, w, l), bf16)],
        compiler_params=_params("parallel", "parallel"),
        name="gdn_intra",
    )(q, k, v, gb, masks)


def _gdn_recur_kernel(s0_ref, uf, wf, qf, kf, tf, gf, ub, wb, qb, kb, tb_, gbk, of_ref, ob_ref, sfin_ref, s_ref):
    n = pl.program_id(0)
    nb = uf.shape[1]
    chunk = uf.shape[2]

    @pl.when(n == 0)
    def _():
        s_ref[...] = s0_ref[...]

    dirs = ((uf, wf, qf, kf, tf, gf, of_ref), (ub, wb, qb, kb, tb_, gbk, ob_ref))
    for d, (u_r, w_r, q_r, qk_r, kt_r, g_r, o_r) in enumerate(dirs):
        for b in range(nb):
            for h in range(GDN_HEADS):
                idx = (d * nb + b) * GDN_HEADS + h
                sl = slice(h * GDN_DK, (h + 1) * GDN_DK)
                c = 4 * GDN_HEADS + d * GDN_HEADS + h
                s = s_ref[idx]
                sb = s.astype(bf16)
                v_new = u_r[0, b, :, sl] - jnp.dot(w_r[0, b, :, sl], sb, preferred_element_type=f32)
                vb = v_new.astype(bf16)
                o = jnp.dot(q_r[0, b, :, sl], sb, preferred_element_type=f32)
                o = o + jnp.dot(qk_r[0, b, :, h * chunk:(h + 1) * chunk], vb, preferred_element_type=f32)
                o_r[b, :, sl] = o
                dec = jnp.exp(g_r[b, 0:1, c:c + 1])
                s_ref[idx] = s * dec + jnp.dot(kt_r[0, b, sl, :], vb, preferred_element_type=f32)

    @pl.when(n == pl.num_programs(0) - 1)
    def _():
        sfin_ref[...] = s_ref[...]


def _gdn_recur(s0, u, w, qd, qk, kt, gb):
    _, b, l, wd = u.shape
    chunk = min(GDN_CHUNK, l)
    nc = l // chunk
    fwd = lambda n: n
    bwd = lambda n: nc - 1 - n

    def specs(order, d):
        tok = lambda width: pl.BlockSpec((1, b, chunk, width), lambda n: (d, 0, order(n), 0))
        return [tok(wd), tok(wd), tok(wd), tok(GDN_HEADS * chunk),
                pl.BlockSpec((1, b, wd, chunk), lambda n: (d, 0, 0, order(n))),
                pl.BlockSpec((b, chunk, LANES), lambda n: (0, order(n), 0))]

    nstate = 2 * b * GDN_HEADS
    state = pl.BlockSpec((nstate, GDN_DK, GDN_DK), lambda n: (0, 0, 0))
    out_tok = lambda order: pl.BlockSpec((b, chunk, wd), lambda n: (0, order(n), 0))
    args = (u, w, qd, qk, kt, gb)
    return pl.pallas_call(
        _gdn_recur_kernel,
        grid=(nc,),
        in_specs=[state] + specs(fwd, 0) + specs(bwd, 1),
        out_specs=[out_tok(fwd), out_tok(bwd), state],
        out_shape=[jax.ShapeDtypeStruct((b, l, wd), f32)] * 2 + [jax.ShapeDtypeStruct((nstate, GDN_DK, GDN_DK), f32)],
        scratch_shapes=[pltpu.VMEM((nstate, GDN_DK, GDN_DK), f32)],
        compiler_params=_params("arbitrary"),
        name="gdn_recur",
    )(s0, *args, *args)


def _dft_mats(n):
    k = np.arange(n)
    ang = 2.0 * np.pi * ((k[:, None] * k[None, :]) % n) / n
    return np.cos(ang), np.sin(ang)


def _channel_dft():
    c, s = _dft_mats(FNET_GDIM)
    eye = np.eye(FNET_GROUPS)
    return jnp.asarray(np.kron(eye, c), f32), jnp.asarray(np.kron(eye, s), f32)


def _fft1_kernel(x_ref, c_ref, s_ref, tc_ref, ts_ref, br_ref, bi_ref):
    x = x_ref[0]
    ar = _dot_f32(c_ref[...], x)
    ai = -_dot_f32(s_ref[...], x)
    tc, ts = tc_ref[...], ts_ref[...]
    br = ar * tc + ai * ts
    bi = ai * tc - ar * ts
    for t in range(br_ref.shape[1]):
        br_ref[0, t] = br[:, t * FNET_W:(t + 1) * FNET_W]
        bi_ref[0, t] = bi[:, t * FNET_W:(t + 1) * FNET_W]


def _fft2_kernel(br_ref, bi_ref, c_ref, s_ref, cc_ref, sc_ref, o_ref, *, scale):
    br, bi = br_ref[0], bi_ref[0]
    c, s = c_ref[...], s_ref[...]
    xr = _dot_f32(c, br) + _dot_f32(s, bi)
    xi = _dot_f32(c, bi) - _dot_f32(s, br)
    for t in range(xr.shape[1] // FNET_W):
        sl = slice(t * FNET_W, (t + 1) * FNET_W)
        o_ref[0, :, sl] = (_dot_f32(xr[:, sl], cc_ref[...]) + _dot_f32(xi[:, sl], sc_ref[...])) * scale


def _fnet_long(f, n1, n2):
    b, l, w = f.shape
    tn = FNET_LANE_TILE
    npos = tn // w
    c1, s1 = _dft_mats(n1)
    c2, s2 = _dft_mats(n2)
    ang = 2.0 * np.pi * ((np.arange(n1)[:, None] * np.arange(n2)[None, :]) % l) / l
    tc = jnp.asarray(np.repeat(np.cos(ang), w, axis=1), f32)
    ts = jnp.asarray(np.repeat(np.sin(ang), w, axis=1), f32)
    cc, sc = _channel_dft()
    const = lambda shape: pl.BlockSpec(shape, lambda bi, ji: (0,) * len(shape))
    br, bi = pl.pallas_call(
        _fft1_kernel,
        grid=(b, n2 * w // tn),
        in_specs=[pl.BlockSpec((1, n1, tn), lambda bi, ji: (bi, 0, ji)),
                  const((n1, n1)), const((n1, n1)),
                  pl.BlockSpec((n1, tn), lambda bi, ji: (0, ji)),
                  pl.BlockSpec((n1, tn), lambda bi, ji: (0, ji))],
        out_specs=[pl.BlockSpec((1, npos, n1, w), lambda bi, ji: (bi, ji, 0, 0))] * 2,
        out_shape=[jax.ShapeDtypeStruct((b, n2, n1, w), f32)] * 2,
        compiler_params=_params("parallel", "parallel"),
        name="fnet_stage1",
    )(f.reshape(b, n1, n2 * w), jnp.asarray(c1, f32), jnp.asarray(s1, f32), tc, ts)
    out = pl.pallas_call(
        functools.partial(_fft2_kernel, scale=1.0 / math.sqrt(l * FNET_GDIM)),
        grid=(b, n1 * w // tn),
        in_specs=[pl.BlockSpec((1, n2, tn), lambda bi, ji: (bi, 0, ji))] * 2
                 + [const((n2, n2)), const((n2, n2)), const((w, w)), const((w, w))],
        out_specs=pl.BlockSpec((1, n2, tn), lambda bi, ji: (bi, 0, ji)),
        out_shape=jax.ShapeDtypeStruct((b, n2, n1 * w), f32),
        compiler_params=_params("parallel", "parallel"),
        name="fnet_stage2",
    )(br.reshape(b, n2, n1 * w), bi.reshape(b, n2, n1 * w),
      jnp.asarray(c2, f32), jnp.asarray(s2, f32), cc, sc)
    return out.reshape(b, l, w)


def _fft_short_kernel(x_ref, c_ref, s_ref, cc_ref, sc_ref, o_ref, *, scale):
    x = x_ref[0]
    zr = _dot_f32(c_ref[...], x)
    zi = -_dot_f32(s_ref[...], x)
    o_ref[0] = (_dot_f32(zr, cc_ref[...]) + _dot_f32(zi, sc_ref[...])) * scale


def _fnet_short(f):
    b, l, w = f.shape
    c, s = _dft_mats(l)
    cc, sc = _channel_dft()
    const = lambda shape: pl.BlockSpec(shape, lambda bi: (0,) * len(shape))
    return pl.pallas_call(
        functools.partial(_fft_short_kernel, scale=1.0 / math.sqrt(l * FNET_GDIM)),
        grid=(b,),
        in_specs=[pl.BlockSpec((1, l, w), lambda bi: (bi, 0, 0)),
                  const((l, l)), const((l, l)), const((w, w)), const((w, w))],
        out_specs=pl.BlockSpec((1, l, w), lambda bi: (bi, 0, 0)),
        out_shape=jax.ShapeDtypeStruct((b, l, w), f32),
        compiler_params=_params("parallel"),
        name="fnet_short",
    )(f, jnp.asarray(c, f32), jnp.asarray(s, f32), cc, sc)


def _fnet(f):
    l = f.shape[1]
    n1 = 1 << (int(math.log2(l)) // 2)
    n2 = l // n1
    if l <= FNET_SHORT_MAX or n2 * FNET_W % FNET_LANE_TILE or n1 * FNET_W % FNET_LANE_TILE:
        return _fnet_short(f)
    return _fnet_long(f, n1, n2)


CONV_HALO = 16


def _conv_kernel(gl_ref, dw_ref, db_ref, gw_ref, gb_ref, avg_ref, o_ref, pad_ref, *, row_len):
    tl = gl_ref.shape[1]
    nr = tl // row_len
    y = gl_ref[0, :, :CONV_W] * jax.nn.sigmoid(gl_ref[0, :, CONV_W:])
    zeros = jnp.zeros((nr, CONV_HALO, CONV_W), f32)
    pad_ref[:, 0:CONV_HALO, :] = zeros
    pad_ref[:, CONV_HALO + row_len:, :] = zeros
    pad_ref[:, CONV_HALO:CONV_HALO + row_len, :] = y.reshape(nr, row_len, CONV_W)
    first = CONV_HALO - CONV_K // 2
    acc = jnp.zeros((nr, row_len, CONV_W), f32)
    for k in range(CONV_K):
        acc = acc + dw_ref[k:k + 1, :].reshape(1, 1, CONV_W) * pad_ref[:, first + k:first + k + row_len, :]
    yc = acc.reshape(tl, CONV_W) + db_ref[...]
    mu = _dot_f32(yc, avg_ref[...])
    cen = yc - mu
    var = _dot_f32(cen * cen, avg_ref[...])
    yn = cen * lax.rsqrt(var + EPS) * gw_ref[...] + gb_ref[...]
    o_ref[0] = _silu(yn)


def _conv_module(gl, dw_w, dw_b, gn_w, gn_b, row_len):
    b, l, w2 = gl.shape
    tl = max(row_len, min(CONV_TILE, l))
    gd = CONV_W // CONV_GROUPS
    avg = jnp.asarray(np.kron(np.eye(CONV_GROUPS), np.full((gd, gd), 1.0 / gd)), f32)
    const = lambda shape: pl.BlockSpec(shape, lambda bi, ji: (0,) * len(shape))
    vec = lambda v: v.reshape(1, CONV_W)
    return pl.pallas_call(
        functools.partial(_conv_kernel, row_len=row_len),
        grid=(b, l // tl),
        in_specs=[pl.BlockSpec((1, tl, w2), lambda bi, ji: (bi, ji, 0)),
                  const((CONV_K, CONV_W)), const((1, CONV_W)), const((1, CONV_W)), const((1, CONV_W)),
                  const((CONV_W, CONV_W))],
        out_specs=pl.BlockSpec((1, tl, CONV_W), lambda bi, ji: (bi, ji, 0)),
        out_shape=jax.ShapeDtypeStruct((b, l, CONV_W), f32),
        scratch_shapes=[pltpu.VMEM((tl // row_len, row_len + 2 * CONV_HALO, CONV_W), f32)],
        compiler_params=_params("parallel", "parallel"),
        name="conv_module",
    )(gl, dw_w, vec(dw_b), vec(gn_w), vec(gn_b), avg)


def _mix_out_kernel(of_ref, ob_ref, z_ref, yf_ref, yc_ref, x_ref, gt_ref, gnw_ref, w_ref, o_ref):
    acc = jnp.dot(yf_ref[0].astype(bf16), w_ref[GDN_W:GDN_W + FNET_W, :], preferred_element_type=f32)
    acc = acc + jnp.dot(yc_ref[0].astype(bf16), w_ref[GDN_W + FNET_W:, :], preferred_element_type=f32)
    for h in range(GDN_HEADS):
        sl = slice(h * GDN_DK, (h + 1) * GDN_DK)
        o = of_ref[0, :, sl] + ob_ref[0, :, sl]
        o = o * lax.rsqrt(jnp.mean(o * o, axis=-1, keepdims=True) + EPS) * gnw_ref[...]
        y = (o * _silu(z_ref[0, :, sl])).astype(bf16)
        acc = acc + jnp.dot(y, w_ref[sl, :], preferred_element_type=f32)
    o_ref[0] = x_ref[0] + gt_ref[0] * acc


def _mix_out(o_f, o_b, z, y_f, y_c, x, gate, gdn_norm, w_out_b):
    b, l, d = x.shape
    tm = min(ROW_TILE, l)
    tok = lambda width: pl.BlockSpec((1, tm, width), lambda i, j: (i, j, 0))
    return pl.pallas_call(
        _mix_out_kernel,
        grid=(b, l // tm),
        in_specs=[tok(GDN_W), tok(GDN_W), tok(GDN_W), tok(FNET_W), tok(CONV_W), tok(d),
                  pl.BlockSpec((1, 1, d), lambda i, j: (i, 0, 0)),
                  pl.BlockSpec((1, GDN_DK), lambda i, j: (0, 0)),
                  pl.BlockSpec((d, d), lambda i, j: (0, 0))],
        out_specs=tok(d),
        out_shape=jax.ShapeDtypeStruct((b, l, d), f32),
        compiler_params=_params("parallel", "parallel"),
        name="mix_out_proj",
    )(o_f, o_b, z, y_f, y_c, x, gate, gdn_norm.reshape(1, GDN_DK), w_out_b)


def _peer_fold_kernel(keys_ref, wq_ref, o_ref):
    o_ref[...] = lax.dot_general(keys_ref[0], wq_ref[...], (((1,), (1,)), ((), ())),
                                 preferred_element_type=f32, precision=lax.Precision.HIGHEST).astype(bf16)


def _peer_fold_keys(peer_wq, peer_keys):
    d, nq = peer_wq.shape
    nhp = nq // PEER_DH
    return pl.pallas_call(
        _peer_fold_kernel,
        grid=(nhp,),
        in_specs=[pl.BlockSpec((1, PEER_NKEYS, PEER_DH), lambda i: (i, 0, 0)),
                  pl.BlockSpec((d, PEER_DH), lambda i: (0, i))],
        out_specs=pl.BlockSpec((PEER_NKEYS, d), lambda i: (i, 0)),
        out_shape=jax.ShapeDtypeStruct((nhp * PEER_NKEYS, d), bf16),
        compiler_params=_params("parallel"),
        name="peer_fold_keys",
    )(peer_keys.reshape(nhp, PEER_NKEYS, PEER_DH), peer_wq)


def _peer_query_kernel(x_ref, nw_ref, sc_ref, sh_ref, wk_ref, h_ref, s_ref):
    hb = _ada_norm(x_ref[0], nw_ref[...], sc_ref[0], sh_ref[0]).astype(bf16)
    h_ref[0] = hb
    s = lax.dot_general(wk_ref[...], hb, (((1,), (1,)), ((), ())), preferred_element_type=f32)
    s_ref[0] = s.reshape(s_ref.shape[1:])


def _peer_query(x, nw, sc, sh, wk_b):
    b, l, d = x.shape
    tm = min(PEER_QUERY_TILE, l)
    nhp = 2 * PEER_HEADS
    vec = pl.BlockSpec((1, 1, d), lambda i, j: (i, 0, 0))
    return pl.pallas_call(
        _peer_query_kernel,
        grid=(b, l // tm),
        in_specs=[pl.BlockSpec((1, tm, d), lambda i, j: (i, j, 0)),
                  pl.BlockSpec((1, d), lambda i, j: (0, 0)), vec, vec,
                  pl.BlockSpec((nhp * PEER_NKEYS, d), lambda i, j: (0, 0))],
        out_specs=[pl.BlockSpec((1, tm, d), lambda i, j: (i, j, 0)),
                   pl.BlockSpec((1, nhp, PEER_NKEYS, tm), lambda i, j: (i, 0, 0, j))],
        out_shape=[jax.ShapeDtypeStruct((b, l, d), bf16),
                   jax.ShapeDtypeStruct((b, nhp, PEER_NKEYS, l), f32)],
        compiler_params=_params("parallel", "parallel"),
        name="peer_query",
    )(x, nw.reshape(1, d), sc, sh, wk_b)


CAND_SHORT = 8
CAND_ROWS = PEER_TOPK + (PEER_TOPK - 1) * CAND_SHORT


def _top_values(cur, out_ref):
    for r in range(PEER_TOPK):
        m = jnp.max(cur, axis=0, keepdims=True)
        out_ref[r:r + 1, :] = m
        if r + 1 < PEER_TOPK:
            cur = jnp.where(cur == m, NEG_INF, cur)
    return m


def _fill_candidates(av, bv, cand_ref):
    cand_ref[0:PEER_TOPK, :] = av[0:1, :] * bv
    for r in range(1, PEER_TOPK):
        lo = PEER_TOPK + (r - 1) * CAND_SHORT
        cand_ref[lo:lo + CAND_SHORT, :] = av[r:r + 1, :] * bv[0:CAND_SHORT, :]


def _peer_topk_kernel(s_ref, a_ref, b_ref, th_ref, sv1_ref, sv2_ref, cand_ref, top_ref):
    def head(h, carry):
        s1 = s_ref[0, 2 * h]
        s2 = s_ref[0, 2 * h + 1]
        thr1 = _top_values(s1, sv1_ref)
        thr2 = _top_values(s2, sv2_ref)
        m1, m2 = sv1_ref[0:1, :], sv2_ref[0:1, :]
        av = jnp.exp(sv1_ref[...] - m1)
        bv = jnp.exp(sv2_ref[...] - m2)
        _fill_candidates(av, bv, cand_ref)
        theta = _top_values(cand_ref[...], top_ref)
        cand = cand_ref[...]
        sel = cand >= theta
        zsum = jnp.sum(jnp.where(sel, cand, 0.0), axis=0, keepdims=True)
        rz = 1.0 / zsum
        _fill_candidates(av * rz, bv, cand_ref)
        th_ref[0, h] = jnp.min(jnp.where(sel, cand_ref[...], jnp.inf), axis=0, keepdims=True)
        a_ref[0, h] = jnp.where(s1 >= thr1, jnp.exp(s1 - m1) * rz, 0.0)
        b_ref[0, h] = jnp.where(s2 >= thr2, jnp.exp(s2 - m2), 0.0)
        return carry

    lax.fori_loop(0, PEER_HEADS, head, 0)


def _peer_topk(scores):
    b, nhp, nk, l = scores.shape
    tt = min(PEER_TOPK_TILE, l)
    ab_spec = pl.BlockSpec((1, PEER_HEADS, nk, tt), lambda i, j: (i, 0, 0, j))
    return pl.pallas_call(
        _peer_topk_kernel,
        grid=(b, l // tt),
        in_specs=[pl.BlockSpec((1, nhp, nk, tt), lambda i, j: (i, 0, 0, j))],
        out_specs=[ab_spec, ab_spec, pl.BlockSpec((1, PEER_HEADS, 1, tt), lambda i, j: (i, 0, 0, j))],
        out_shape=[jax.ShapeDtypeStruct((b, PEER_HEADS, nk, l), f32)] * 2
                  + [jax.ShapeDtypeStruct((b, PEER_HEADS, 1, l), f32)],
        scratch_shapes=[pltpu.VMEM((PEER_TOPK, tt), f32), pltpu.VMEM((PEER_TOPK, tt), f32),
                        pltpu.VMEM((CAND_ROWS, tt), f32), pltpu.VMEM((PEER_TOPK, tt), f32)],
        compiler_params=_params("parallel", "parallel"),
        name="peer_topk",
    )(scores)


def _gelu_tanh(x):
    return 0.5 * x * (1.0 + jnp.tanh(math.sqrt(2.0 / math.pi) * (x + 0.044715 * (x * x * x))))


def _peer_dense_kernel(h_ref, a_ref, b_ref, th_ref, u_ref, vt_ref, x_ref, gt_ref, o_ref,
                       acc_ref, s_ref, g_ref):
    e = pl.program_id(2)
    tt = h_ref.shape[1]
    te = u_ref.shape[0]
    keys_per_sub = PEER_SUB_TILE // PEER_NKEYS

    @pl.when(e == 0)
    def _():
        acc_ref[...] = jnp.zeros_like(acc_ref)

    n_sub = te // PEER_SUB_TILE
    subs = [slice(k * PEER_SUB_TILE, (k + 1) * PEER_SUB_TILE) for k in range(n_sub)]

    def scores(k):
        s_ref[subs[k], :] = lax.dot_general(u_ref[subs[k], :], h_ref[0], (((1,), (1,)), ((), ())),
                                            preferred_element_type=f32)

    def project(k):
        acc_ref[...] += jnp.dot(vt_ref[:, subs[k]], g_ref[subs[k], :], preferred_element_type=f32)

    scores(0)
    for k in range(n_sub):
        if k + 1 < n_sub:
            scores(k + 1)
        if k > 0:
            project(k - 1)
        for ii in range(k * keys_per_sub, (k + 1) * keys_per_sub):
            rows = slice(ii * PEER_NKEYS, (ii + 1) * PEER_NKEYS)
            for lc in range(tt // PEER_LANE_CHUNK):
                lanes = slice(lc * PEER_LANE_CHUNK, (lc + 1) * PEER_LANE_CHUNK)
                wsum = jnp.zeros((PEER_NKEYS, PEER_LANE_CHUNK), f32)
                for h in range(PEER_HEADS):
                    p = a_ref[0, h, ii:ii + 1, lanes] * b_ref[0, h, :, lanes]
                    wsum = wsum + jnp.where(p >= th_ref[0, h, :, lanes], p, 0.0)
                g_ref[rows, lanes] = (_gelu_tanh(s_ref[rows, lanes]) * wsum).astype(bf16)
    project(n_sub - 1)

    @pl.when(e == pl.num_programs(2) - 1)
    def _():
        o_ref[0] = x_ref[0] + gt_ref[0] * acc_ref[...].T


def _peer_dense(hb, a, bsel, theta, u_b, vt_b, x, gate):
    b, l, d = x.shape
    ne = u_b.shape[0]
    tt = min(PEER_TOKEN_TILE, l)
    te = PEER_EXPERT_TILE
    nfirst = te // PEER_NKEYS
    return pl.pallas_call(
        _peer_dense_kernel,
        grid=(b, l // tt, ne // te),
        in_specs=[pl.BlockSpec((1, tt, d), lambda i, j, e: (i, j, 0)),
                  pl.BlockSpec((1, PEER_HEADS, nfirst, tt), lambda i, j, e: (i, 0, e, j)),
                  pl.BlockSpec((1, PEER_HEADS, PEER_NKEYS, tt), lambda i, j, e: (i, 0, 0, j)),
                  pl.BlockSpec((1, PEER_HEADS, 1, tt), lambda i, j, e: (i, 0, 0, j)),
                  pl.BlockSpec((te, d), lambda i, j, e: (e, 0)),
                  pl.BlockSpec((d, te), lambda i, j, e: (0, e)),
                  pl.BlockSpec((1, tt, d), lambda i, j, e: (i, j, 0)),
                  pl.BlockSpec((1, 1, d), lambda i, j, e: (i, 0, 0))],
        out_specs=pl.BlockSpec((1, tt, d), lambda i, j, e: (i, j, 0)),
        out_shape=jax.ShapeDtypeStruct((b, l, d), f32),
        scratch_shapes=[pltpu.VMEM((d, tt), f32), pltpu.VMEM((te, tt), f32), pltpu.VMEM((te, tt), bf16)],
        compiler_params=_params("parallel", "parallel", "arbitrary"),
        name="peer_dense",
    )(hb, a, bsel, theta, u_b, vt_b, x, gate)


def _final_norm_kernel(x_ref, w_ref, o_ref):
    x = x_ref[0]
    o_ref[0] = x * lax.rsqrt(jnp.mean(x * x, axis=-1, keepdims=True) + EPS) * w_ref[...]


def _final_norm(x, w):
    b, l, d = x.shape
    tm = min(2 * ROW_TILE, l)
    return pl.pallas_call(
        _final_norm_kernel,
        grid=(b, l // tm),
        in_specs=[pl.BlockSpec((1, tm, d), lambda i, j: (i, j, 0)), pl.BlockSpec((1, d), lambda i, j: (0, 0))],
        out_specs=pl.BlockSpec((1, tm, d), lambda i, j: (i, j, 0)),
        out_shape=jax.ShapeDtypeStruct((b, l, d), f32),
        compiler_params=_params("parallel", "parallel"),
        name="final_norm",
    )(x, w.reshape(1, d))


def _gdn(qkv, ab, s0, lw):
    q, k, v, gb = _gdn_prep(qkv, ab, lw["conv_qkv"], lw["a_log"], lw["dt_bias"])
    return _gdn_recur(s0, *_gdn_intra(q, k, v, gb), gb)


def _mixer(x, proj, o_f, o_b, gate, row_len, lw):
    _, z, f, cgl, _ = proj
    y_f = _fnet(f)
    y_c = _conv_module(cgl, lw["dw_w"], lw["dw_b"], lw["gn_w"], lw["gn_b"], row_len)
    return _mix_out(o_f, o_b, z, y_f, y_c, x, gate, lw["gdn_norm"], lw["w_out"])


def _peer(x, sc, sh, gate, lw):
    hb, scores = _peer_query(x, lw["norm2"], sc, sh, lw["peer_wk"])
    a, bsel, theta = _peer_topk(scores)
    return _peer_dense(hb, a, bsel, theta, lw["peer_u"], lw["peer_vt"], x, gate)


def kernel(x, c, ctx, c_ctx, w_mod, b_mod, norm1, norm2, w_in, conv_qkv, a_log, dt_bias, gdn_norm, dw_w, dw_b,
           gn_w, gn_b, w_out, peer_wq, peer_keys, peer_u, peer_v, final_norm):
    bsz, _, d = x.shape
    depth = w_mod.shape[0]
    cc = jnp.zeros((SUBLANES, d), f32).at[:bsz].set(c).at[bsz].set(c_ctx)
    mods = _modulation(cc, w_mod, b_mod)
    xc = ctx
    for i in range(depth):
        last = i == depth - 1
        lw = {
            "norm2": norm2[i], "conv_qkv": conv_qkv[i], "a_log": a_log[i], "dt_bias": dt_bias[i],
            "gdn_norm": gdn_norm[i], "dw_w": dw_w[i], "dw_b": dw_b[i], "gn_w": gn_w[i], "gn_b": gn_b[i],
            "w_out": w_out[i].astype(bf16), "peer_wk": _peer_fold_keys(peer_wq[i], peer_keys[i]),
            "peer_u": peer_u[i].astype(bf16), "peer_vt": peer_v[i].T.astype(bf16),
        }
        w_in_p = _pack_w_in(w_in[i])
        mod = mods[i, :bsz].reshape(bsz, 6, 1, d)
        mod_c = jnp.broadcast_to(mods[i, bsz].reshape(1, 6, 1, d), (bsz, 6, 1, d))
        sh1, sc1, gt1, sh2, sc2, gt2 = (mod[:, j] for j in range(6))
        csh1, csc1, cgt1, csh2, csc2, cgt2 = (mod_c[:, j] for j in range(6))

        pc = _norm_in(xc, norm1[i], csc1, csh1, w_in_p)
        p = _norm_in(x, norm1[i], sc1, sh1, w_in_p)
        zero = jnp.zeros((2 * bsz * GDN_HEADS, GDN_DK, GDN_DK), f32)
        oc_f, oc_b, s_ctx = _gdn(pc[0], pc[4], zero, lw)
        o_f, o_b, _ = _gdn(p[0], p[4], s_ctx, lw)
        x = _mixer(x, p, o_f, o_b, gt1, GRID_W, lw)
        x = _peer(x, sc2, sh2, gt2, lw)
        if not last:
            xc = _mixer(xc, pc, oc_f, oc_b, cgt1, xc.shape[1], lw)
            xc = _peer(xc, csc2, csh2, cgt2, lw)
    return _final_norm(x, final_norm)
```

```python
import functools
import math

import numpy as np
import jax
import jax.numpy as jnp
from jax import lax
from jax.experimental import pallas as pl
from jax.experimental.pallas import tpu as pltpu

f32 = jnp.float32
bf16 = jnp.bfloat16

D_MODEL = 1024
DEPTH = 2
GRID_W = 64
GDN_DK = 128
GDN_HEADS = 4
GDN_W = 512
FNET_W = 256
FNET_GROUPS = 4
FNET_GDIM = 64
CONV_W = 256
CONV_GROUPS = 4
CONV_K = 31
QKV_OFF, Z_OFF, A_OFF, B_OFF, F_OFF, C_OFF, IN_W = 0, 1536, 2048, 2056, 2064, 2320, 2832
PEER_HEADS = 8
PEER_NKEYS = 128
PEER_TOPK = 16
PEER_DH = 128
EPS = 1e-6

LANES = 128
SUBLANES = 8
VMEM_LIMIT_BYTES = 56 * 2**20

GDN_CHUNK = 128
GDN_BLOCK = 256
GDN_SUB = 16
ROW_TILE = 256
CONV_TILE = 512
FNET_LANE_TILE = 8 * FNET_W
FNET_SHORT_MAX = 512
PEER_QUERY_TILE = 512
PEER_TOKEN_TILE = 512
PEER_EXPERT_TILE = 2048
PEER_SUB_TILE = 256
PEER_TOPK_TILE = 256
PEER_LANE_CHUNK = 128
PEER_ROW_CHUNK = 64
PEER_PROJ_CHUNK = 256
GELU_HALF = 0.5
GELU_C1 = math.sqrt(2.0 / math.pi)


def _params(*sem):
    return pltpu.CompilerParams(dimension_semantics=sem, vmem_limit_bytes=VMEM_LIMIT_BYTES)


def _dot(a, b):
    return jnp.dot(a.astype(bf16), b.astype(bf16), preferred_element_type=f32)


def _dot_nt(a, b):
    return lax.dot_general(a.astype(bf16), b.astype(bf16), (((1,), (1,)), ((), ())),
                           preferred_element_type=f32)


def _dot_f32(a, b):
    return jnp.dot(a, b, preferred_element_type=f32, precision=lax.Precision.HIGHEST)


def _silu(x):
    return x * jax.nn.sigmoid(x)


def _mod_kernel(c_ref, w_ref, b_ref, o_ref):
    o_ref[0] = _dot(_silu(c_ref[...]), w_ref[0]) + b_ref[0]


def _modulation(cc, w_mod, b_mod):
    depth, d, n = w_mod.shape
    tn = n // 4
    return pl.pallas_call(
        _mod_kernel,
        grid=(depth, n // tn),
        in_specs=[pl.BlockSpec((SUBLANES, d), lambda i, j: (0, 0)),
                  pl.BlockSpec((1, d, tn), lambda i, j: (i, 0, j)),
                  pl.BlockSpec((1, 1, tn), lambda i, j: (i, 0, j))],
        out_specs=pl.BlockSpec((1, SUBLANES, tn), lambda i, j: (i, 0, j)),
        out_shape=jax.ShapeDtypeStruct((depth, SUBLANES, n), f32),
        compiler_params=_params("parallel", "parallel"),
        name="modulation",
    )(cc, w_mod, b_mod.reshape(depth, 1, n))


def _ada_norm(x, nw, sc, sh):
    r = lax.rsqrt(jnp.mean(x * x, axis=-1, keepdims=True) + EPS)
    return (x * r * nw) * (1.0 + sc) + sh


IN_SPLITS = (3 * GDN_W, GDN_W, FNET_W, 2 * CONV_W, LANES)


def _norm_in_kernel(x_ref, nw_ref, sc_ref, sh_ref, w_ref, *out_refs):
    hb = _ada_norm(x_ref[0], nw_ref[...], sc_ref[0], sh_ref[0]).astype(bf16)
    off = 0
    for o_ref, width in zip(out_refs, IN_SPLITS):
        o_ref[0] = jnp.dot(hb, w_ref[:, off:off + width], preferred_element_type=f32)
        off += width


def _pack_w_in(w_in):
    ab = jnp.pad(w_in[:, A_OFF:F_OFF], ((0, 0), (0, LANES - (F_OFF - A_OFF))))
    return jnp.concatenate([w_in[:, QKV_OFF:A_OFF], w_in[:, F_OFF:IN_W], ab], axis=1).astype(bf16)


def _norm_in(x, nw, sc, sh, w_packed):
    b, l, d = x.shape
    tm = min(ROW_TILE, l)
    n = w_packed.shape[1]
    vec = pl.BlockSpec((1, 1, d), lambda i, j: (i, 0, 0))
    return pl.pallas_call(
        _norm_in_kernel,
        grid=(b, l // tm),
        in_specs=[pl.BlockSpec((1, tm, d), lambda i, j: (i, j, 0)),
                  pl.BlockSpec((1, d), lambda i, j: (0, 0)),
                  vec, vec,
                  pl.BlockSpec((d, n), lambda i, j: (0, 0))],
        out_specs=[pl.BlockSpec((1, tm, w), lambda i, j: (i, j, 0)) for w in IN_SPLITS],
        out_shape=[jax.ShapeDtypeStruct((b, l, w), f32) for w in IN_SPLITS],
        compiler_params=_params("parallel", "parallel"),
        name="norm_in_proj",
    )(x, nw.reshape(1, d), sc, sh, w_packed)


def _split3(x):
    x1 = x.astype(bf16)
    r1 = x - x1.astype(f32)
    x2 = r1.astype(bf16)
    x3 = (r1 - x2.astype(f32)).astype(bf16)
    return x1, x2, x3


def _dot01(m_ref, parts):
    m = m_ref[...]
    return sum(jnp.dot(m, p, preferred_element_type=f32) for p in parts)


def _gdn_prep_kernel(prev_ref, x_ref, next_ref, ab_ref, cw_ref, alog_ref, dtb_ref,
                     tril_ref, triu_ref, same_ref, q_ref, k_ref, v_ref, gb_ref):
    j = pl.program_id(1)
    tl = x_ref.shape[1]
    first = j == 0
    last = j == pl.num_programs(1) - 1
    row = lax.broadcasted_iota(jnp.int32, (tl, LANES), 0)
    outs = (q_ref, k_ref, v_ref)
    for part in range(3):
        for h in range(GDN_HEADS):
            lo = part * GDN_W + h * GDN_DK
            sl = slice(lo, lo + GDN_DK)
            x = x_ref[0, :, sl]
            prev_row = jnp.where(first, 0.0, prev_ref[0, SUBLANES - 1:SUBLANES, sl])
            next_row = jnp.where(last, 0.0, next_ref[0, 0:1, sl])
            xp = jnp.where(row == 0, prev_row, pltpu.roll(x, 1, axis=0))
            xn = jnp.where(row == tl - 1, next_row, pltpu.roll(x, tl - 1, axis=0))
            y = _silu(cw_ref[0:1, sl] * xp + cw_ref[1:2, sl] * x + cw_ref[2:3, sl] * xn)
            if part < 2:
                y = y * lax.rsqrt(jnp.sum(y * y, axis=-1, keepdims=True) + EPS)
            if part == 0:
                y = y * (GDN_DK ** -0.5)
            outs[part][0, :, h * GDN_DK:(h + 1) * GDN_DK] = y
    ab = ab_ref[0]
    col = lax.broadcasted_iota(jnp.int32, ab.shape, 1)
    nh2 = 2 * GDN_HEADS
    sp = jnp.maximum(ab + dtb_ref[...], 0.0) + jnp.log1p(jnp.exp(-jnp.abs(ab + dtb_ref[...])))
    g = jnp.where(col < nh2, -jnp.exp(alog_ref[...]) * sp, 0.0)
    parts = _split3(g)
    g_f = _dot01(tril_ref, parts)
    g_b = _dot01(triu_ref, parts)
    tot = _dot01(same_ref, parts)
    beta = jax.nn.sigmoid(ab)
    out = jnp.where(col < GDN_HEADS, g_f, jnp.where(col < nh2, g_b, 0.0))
    out = out + jnp.where((col >= nh2) & (col < 2 * nh2), beta, 0.0)
    out = out + pltpu.roll(tot, 2 * nh2, axis=1)
    gb_ref[0] = out


def _chunk_masks(tb, chunk):
    i = np.arange(tb)[:, None]
    j = np.arange(tb)[None, :]
    same = (i // chunk) == (j // chunk)
    return same, i, j


def _gdn_prep(qkv, ab, conv_qkv, a_log, dt_bias):
    b, l, w = qkv.shape
    tl = min(GDN_BLOCK, l)
    nblk = tl // SUBLANES
    same, i, j = _chunk_masks(tl, GDN_CHUNK)
    tril = jnp.asarray(same & (i >= j), bf16)
    triu = jnp.asarray(same & (i <= j), bf16)
    samem = jnp.asarray(same, bf16)
    row = lambda v: jnp.pad(v.reshape(1, -1).astype(f32), ((0, 0), (0, LANES - v.size)))
    nlast = l // SUBLANES - 1
    const = lambda shape: pl.BlockSpec(shape, lambda bi, ji: (0,) * len(shape))
    tok = lambda width: pl.BlockSpec((1, tl, width), lambda bi, ji: (bi, ji, 0))
    return pl.pallas_call(
        _gdn_prep_kernel,
        grid=(b, l // tl),
        in_specs=[pl.BlockSpec((1, SUBLANES, w), lambda bi, ji: (bi, jnp.maximum(ji * nblk - 1, 0), 0)),
                  tok(w),
                  pl.BlockSpec((1, SUBLANES, w), lambda bi, ji: (bi, jnp.minimum((ji + 1) * nblk, nlast), 0)),
                  tok(LANES),
                  const((3, w)), const((1, LANES)), const((1, LANES)),
                  const((tl, tl)), const((tl, tl)), const((tl, tl))],
        out_specs=[tok(GDN_W), tok(GDN_W), tok(GDN_W), tok(LANES)],
        out_shape=[jax.ShapeDtypeStruct((b, l, GDN_W), f32)] * 3 + [jax.ShapeDtypeStruct((b, l, LANES), f32)],
        compiler_params=_params("parallel", "parallel"),
        name="gdn_prep",
    )(qkv, qkv, qkv, ab, conv_qkv, row(a_log), row(dt_bias), tril, triu, samem)


def _neumann_inverse(a_list, sub_mask, n_sub):
    d = [a * sub_mask for a in a_list]
    e = [a - x for a, x in zip(a_list, d)]
    n = [-x for x in d]
    p = n
    for _ in range(int(math.log2(GDN_SUB)) - 1):
        p = [_dot(x, x) for x in p]
        n = [nn + pp + _dot(nn, pp) for nn, pp in zip(n, p)]
    q = [-(ee + _dot(nn, ee)) for nn, ee in zip(n, e)]
    y, p = q, q
    for _ in range(int(math.log2(n_sub)) - 1):
        p = [_dot(x, x) for x in p]
        y = [yy + pp + _dot(yy, pp) for yy, pp in zip(y, p)]
    return [yy + nn + _dot(yy, nn) for yy, nn in zip(y, n)]


def _gdn_intra_kernel(q_ref, k_ref, v_ref, gb_ref, m_ref, u_ref, w_ref, qd_ref, qk_ref, kt_ref):
    tb = q_ref.shape[1]
    chunk = min(GDN_CHUNK, tb)
    gb = gb_ref[0]
    gbt = gb.T
    nh2 = 2 * GDN_HEADS
    chains, a_list, rhs_list = [], [], []
    for h in range(GDN_HEADS):
        sl = slice(h * GDN_DK, (h + 1) * GDN_DK)
        qh, kh, vh = q_ref[0, :, sl], k_ref[0, :, sl], v_ref[0, :, sl]
        kk = _dot_nt(kh, kh)
        qk = _dot_nt(qh, kh)
        kht = kh.T
        for d in range(2):
            c = d * GDN_HEADS + h
            g_col, g_row = gb[:, c:c + 1], gbt[c:c + 1, :]
            beta = gb[:, nh2 + c:nh2 + c + 1]
            tot_row = gbt[2 * nh2 + c:2 * nh2 + c + 1, :]
            decay = jnp.exp(jnp.minimum(g_col - g_row, 0.0))
            eg = jnp.exp(g_col)
            kb = kh * beta
            chains.append((d, sl))
            a_list.append(kk * beta * decay * m_ref[2 * d])
            rhs_list.append(jnp.concatenate([vh * beta, kb * eg], axis=1))
            qd_ref[d, 0, :, sl] = (qh * eg).astype(bf16)
            qkm = qk * decay * m_ref[2 * d + 1]
            for ci in range(tb // chunk):
                rs = slice(ci * chunk, (ci + 1) * chunk)
                qk_ref[d, 0, rs, h * chunk:(h + 1) * chunk] = qkm[rs, rs].astype(bf16)
            kt_ref[d, 0, sl, :] = (kht * jnp.exp(tot_row - g_row)).astype(bf16)
    nt_list = _neumann_inverse(a_list, m_ref[4], chunk // GDN_SUB)
    for (d, sl), nt, rhs in zip(chains, nt_list, rhs_list):
        sol = rhs + _dot(nt, rhs)
        u_ref[d, 0, :, sl] = sol[:, :GDN_DK]
        w_ref[d, 0, :, sl] = sol[:, GDN_DK:].astype(bf16)


def _gdn_intra(q, k, v, gb):
    b, l, w = q.shape
    tb = min(GDN_BLOCK, l)
    chunk = min(GDN_CHUNK, tb)
    same, i, j = _chunk_masks(tb, chunk)
    sub = (i // GDN_SUB) == (j // GDN_SUB)
    masks = jnp.asarray(np.stack([same & (i > j), same & (i >= j), same & (i < j), same & (i <= j), sub]), f32)
    tok = lambda width: pl.BlockSpec((1, tb, width), lambda bi, ji: (bi, ji, 0))
    dtok = lambda width: pl.BlockSpec((2, 1, tb, width), lambda bi, ji: (0, bi, ji, 0))
    sds = lambda width, dt: jax.ShapeDtypeStruct((2, b, l, width), dt)
    return pl.pallas_call(
        _gdn_intra_kernel,
        grid=(b, l // tb),
        in_specs=[tok(w), tok(w), tok(w), tok(LANES),
                  pl.BlockSpec((5, tb, tb), lambda bi, ji: (0, 0, 0))],
        out_specs=[dtok(w), dtok(w), dtok(w), dtok(GDN_HEADS * chunk),
                   pl.BlockSpec((2, 1, w, tb), lambda bi, ji: (0, bi, 0, ji))],
        out_shape=[sds(w, f32), sds(w, bf16), sds(w, bf16), sds(GDN_HEADS * chunk, bf16),
                   jax.ShapeDtypeStruct((2, b, w, l), bf16)],
        compiler_params=_params("parallel", "parallel"),
        name="gdn_intra",
    )(q, k, v, gb, masks)


def _gdn_recur_kernel(s0_ref, uf, wf, qf, kf, tf, gf, ub, wb, qb, kb, tb_, gbk, of_ref, ob_ref, sfin_ref, s_ref):
    n = pl.program_id(0)
    nb = uf.shape[1]
    chunk = uf.shape[2]

    @pl.when(n == 0)
    def _():
        s_ref[...] = s0_ref[...]

    dirs = ((uf, wf, qf, kf, tf, gf, of_ref), (ub, wb, qb, kb, tb_, gbk, ob_ref))
    chains = [(d, b, h) for d in range(2) for b in range(nb) for h in range(GDN_HEADS)]
    head = lambda h: slice(h * GDN_DK, (h + 1) * GDN_DK)
    sbs, vbs = [], []
    for idx, (d, b, h) in enumerate(chains):
        u_r, w_r = dirs[d][0], dirs[d][1]
        sb = s_ref[idx].astype(bf16)
        v_new = u_r[0, b, :, head(h)] - jnp.dot(w_r[0, b, :, head(h)], sb, preferred_element_type=f32)
        sbs.append(sb)
        vbs.append(v_new.astype(bf16))
    for idx, (d, b, h) in enumerate(chains):
        q_r, qk_r, o_r = dirs[d][2], dirs[d][3], dirs[d][6]
        o = jnp.dot(q_r[0, b, :, head(h)], sbs[idx], preferred_element_type=f32)
        o_r[b, :, head(h)] = o + jnp.dot(qk_r[0, b, :, h * chunk:(h + 1) * chunk], vbs[idx],
                                         preferred_element_type=f32)
    for idx, (d, b, h) in enumerate(chains):
        kt_r, g_r = dirs[d][4], dirs[d][5]
        c = 4 * GDN_HEADS + d * GDN_HEADS + h
        dec = jnp.exp(g_r[b, 0:1, c:c + 1])
        s_ref[idx] = s_ref[idx] * dec + jnp.dot(kt_r[0, b, head(h), :], vbs[idx], preferred_element_type=f32)

    @pl.when(n == pl.num_programs(0) - 1)
    def _():
        sfin_ref[...] = s_ref[...]


def _gdn_recur(s0, u, w, qd, qk, kt, gb):
    _, b, l, wd = u.shape
    chunk = min(GDN_CHUNK, l)
    nc = l // chunk
    fwd = lambda n: n
    bwd = lambda n: nc - 1 - n

    def specs(order, d):
        tok = lambda width: pl.BlockSpec((1, b, chunk, width), lambda n: (d, 0, order(n), 0))
        return [tok(wd), tok(wd), tok(wd), tok(GDN_HEADS * chunk),
                pl.BlockSpec((1, b, wd, chunk), lambda n: (d, 0, 0, order(n))),
                pl.BlockSpec((b, chunk, LANES), lambda n: (0, order(n), 0))]

    nstate = 2 * b * GDN_HEADS
    state = pl.BlockSpec((nstate, GDN_DK, GDN_DK), lambda n: (0, 0, 0))
    out_tok = lambda order: pl.BlockSpec((b, chunk, wd), lambda n: (0, order(n), 0))
    args = (u, w, qd, qk, kt, gb)
    return pl.pallas_call(
        _gdn_recur_kernel,
        grid=(nc,),
        in_specs=[state] + specs(fwd, 0) + specs(bwd, 1),
        out_specs=[out_tok(fwd), out_tok(bwd), state],
        out_shape=[jax.ShapeDtypeStruct((b, l, wd), f32)] * 2 + [jax.ShapeDtypeStruct((nstate, GDN_DK, GDN_DK), f32)],
        scratch_shapes=[pltpu.VMEM((nstate, GDN_DK, GDN_DK), f32)],
        compiler_params=_params("arbitrary"),
        name="gdn_recur",
    )(s0, *args, *args)


def _dft_mats(n):
    k = np.arange(n)
    ang = 2.0 * np.pi * ((k[:, None] * k[None, :]) % n) / n
    return np.cos(ang), np.sin(ang)


def _channel_dft():
    c, s = _dft_mats(FNET_GDIM)
    eye = np.eye(FNET_GROUPS)
    return jnp.asarray(np.kron(eye, c), f32), jnp.asarray(np.kron(eye, s), f32)


def _fft1_kernel(x_ref, c_ref, s_ref, tc_ref, ts_ref, br_ref, bi_ref):
    x = x_ref[0]
    ar = _dot_f32(c_ref[...], x)
    ai = -_dot_f32(s_ref[...], x)
    tc, ts = tc_ref[...], ts_ref[...]
    br = ar * tc + ai * ts
    bi = ai * tc - ar * ts
    for t in range(br_ref.shape[1]):
        br_ref[0, t] = br[:, t * FNET_W:(t + 1) * FNET_W]
        bi_ref[0, t] = bi[:, t * FNET_W:(t + 1) * FNET_W]


def _fft2_kernel(br_ref, bi_ref, c_ref, s_ref, cc_ref, sc_ref, o_ref, *, scale):
    br, bi = br_ref[0], bi_ref[0]
    c, s = c_ref[...], s_ref[...]
    xr = _dot_f32(c, br) + _dot_f32(s, bi)
    xi = _dot_f32(c, bi) - _dot_f32(s, br)
    for t in range(xr.shape[1] // FNET_W):
        sl = slice(t * FNET_W, (t + 1) * FNET_W)
        o_ref[0, :, sl] = (_dot_f32(xr[:, sl], cc_ref[...]) + _dot_f32(xi[:, sl], sc_ref[...])) * scale


def _fnet_long(f, n1, n2):
    b, l, w = f.shape
    tn = FNET_LANE_TILE
    npos = tn // w
    c1, s1 = _dft_mats(n1)
    c2, s2 = _dft_mats(n2)
    ang = 2.0 * np.pi * ((np.arange(n1)[:, None] * np.arange(n2)[None, :]) % l) / l
    tc = jnp.asarray(np.repeat(np.cos(ang), w, axis=1), f32)
    ts = jnp.asarray(np.repeat(np.sin(ang), w, axis=1), f32)
    cc, sc = _channel_dft()
    const = lambda shape: pl.BlockSpec(shape, lambda bi, ji: (0,) * len(shape))
    br, bi = pl.pallas_call(
        _fft1_kernel,
        grid=(b, n2 * w // tn),
        in_specs=[pl.BlockSpec((1, n1, tn), lambda bi, ji: (bi, 0, ji)),
                  const((n1, n1)), const((n1, n1)),
                  pl.BlockSpec((n1, tn), lambda bi, ji: (0, ji)),
                  pl.BlockSpec((n1, tn), lambda bi, ji: (0, ji))],
        out_specs=[pl.BlockSpec((1, npos, n1, w), lambda bi, ji: (bi, ji, 0, 0))] * 2,
        out_shape=[jax.ShapeDtypeStruct((b, n2, n1, w), f32)] * 2,
        compiler_params=_params("parallel", "parallel"),
        name="fnet_stage1",
    )(f.reshape(b, n1, n2 * w), jnp.asarray(c1, f32), jnp.asarray(s1, f32), tc, ts)
    out = pl.pallas_call(
        functools.partial(_fft2_kernel, scale=1.0 / math.sqrt(l * FNET_GDIM)),
        grid=(b, n1 * w // tn),
        in_specs=[pl.BlockSpec((1, n2, tn), lambda bi, ji: (bi, 0, ji))] * 2
                 + [const((n2, n2)), const((n2, n2)), const((w, w)), const((w, w))],
        out_specs=pl.BlockSpec((1, n2, tn), lambda bi, ji: (bi, 0, ji)),
        out_shape=jax.ShapeDtypeStruct((b, n2, n1 * w), f32),
        compiler_params=_params("parallel", "parallel"),
        name="fnet_stage2",
    )(br.reshape(b, n2, n1 * w), bi.reshape(b, n2, n1 * w),
      jnp.asarray(c2, f32), jnp.asarray(s2, f32), cc, sc)
    return out.reshape(b, l, w)


def _fft_short_kernel(x_ref, c_ref, s_ref, cc_ref, sc_ref, o_ref, *, scale):
    x = x_ref[0]
    zr = _dot_f32(c_ref[...], x)
    zi = -_dot_f32(s_ref[...], x)
    o_ref[0] = (_dot_f32(zr, cc_ref[...]) + _dot_f32(zi, sc_ref[...])) * scale


def _fnet_short(f):
    b, l, w = f.shape
    c, s = _dft_mats(l)
    cc, sc = _channel_dft()
    const = lambda shape: pl.BlockSpec(shape, lambda bi: (0,) * len(shape))
    return pl.pallas_call(
        functools.partial(_fft_short_kernel, scale=1.0 / math.sqrt(l * FNET_GDIM)),
        grid=(b,),
        in_specs=[pl.BlockSpec((1, l, w), lambda bi: (bi, 0, 0)),
                  const((l, l)), const((l, l)), const((w, w)), const((w, w))],
        out_specs=pl.BlockSpec((1, l, w), lambda bi: (bi, 0, 0)),
        out_shape=jax.ShapeDtypeStruct((b, l, w), f32),
        compiler_params=_params("parallel"),
        name="fnet_short",
    )(f, jnp.asarray(c, f32), jnp.asarray(s, f32), cc, sc)


def _fnet(f):
    l = f.shape[1]
    n1 = 1 << (int(math.log2(l)) // 2)
    n2 = l // n1
    if l <= FNET_SHORT_MAX or n2 * FNET_W % FNET_LANE_TILE or n1 * FNET_W % FNET_LANE_TILE:
        return _fnet_short(f)
    return _fnet_long(f, n1, n2)


CONV_HALO = 16


def _conv_kernel(gl_ref, dw_ref, db_ref, gw_ref, gb_ref, avg_ref, o_ref, pad_ref, *, row_len):
    tl = gl_ref.shape[1]
    nr = tl // row_len
    y = gl_ref[0, :, :CONV_W] * jax.nn.sigmoid(gl_ref[0, :, CONV_W:])
    zeros = jnp.zeros((nr, CONV_HALO, CONV_W), f32)
    pad_ref[:, 0:CONV_HALO, :] = zeros
    pad_ref[:, CONV_HALO + row_len:, :] = zeros
    pad_ref[:, CONV_HALO:CONV_HALO + row_len, :] = y.reshape(nr, row_len, CONV_W)
    first = CONV_HALO - CONV_K // 2
    acc = jnp.zeros((nr, row_len, CONV_W), f32)
    for k in range(CONV_K):
        acc = acc + dw_ref[k:k + 1, :].reshape(1, 1, CONV_W) * pad_ref[:, first + k:first + k + row_len, :]
    yc = acc.reshape(tl, CONV_W) + db_ref[...]
    mu = _dot_f32(yc, avg_ref[...])
    cen = yc - mu
    var = _dot_f32(cen * cen, avg_ref[...])
    yn = cen * lax.rsqrt(var + EPS) * gw_ref[...] + gb_ref[...]
    o_ref[0] = _silu(yn)


def _conv_module(gl, dw_w, dw_b, gn_w, gn_b, row_len):
    b, l, w2 = gl.shape
    tl = max(row_len, min(CONV_TILE, l))
    gd = CONV_W // CONV_GROUPS
    avg = jnp.asarray(np.kron(np.eye(CONV_GROUPS), np.full((gd, gd), 1.0 / gd)), f32)
    const = lambda shape: pl.BlockSpec(shape, lambda bi, ji: (0,) * len(shape))
    vec = lambda v: v.reshape(1, CONV_W)
    return pl.pallas_call(
        functools.partial(_conv_kernel, row_len=row_len),
        grid=(b, l // tl),
        in_specs=[pl.BlockSpec((1, tl, w2), lambda bi, ji: (bi, ji, 0)),
                  const((CONV_K, CONV_W)), const((1, CONV_W)), const((1, CONV_W)), const((1, CONV_W)),
                  const((CONV_W, CONV_W))],
        out_specs=pl.BlockSpec((1, tl, CONV_W), lambda bi, ji: (bi, ji, 0)),
        out_shape=jax.ShapeDtypeStruct((b, l, CONV_W), f32),
        scratch_shapes=[pltpu.VMEM((tl // row_len, row_len + 2 * CONV_HALO, CONV_W), f32)],
        compiler_params=_params("parallel", "parallel"),
        name="conv_module",
    )(gl, dw_w, vec(dw_b), vec(gn_w), vec(gn_b), avg)


def _mix_out_kernel(of_ref, ob_ref, z_ref, yf_ref, yc_ref, x_ref, gt_ref, gnw_ref, w_ref, o_ref):
    acc = jnp.dot(yf_ref[0].astype(bf16), w_ref[GDN_W:GDN_W + FNET_W, :], preferred_element_type=f32)
    acc = acc + jnp.dot(yc_ref[0].astype(bf16), w_ref[GDN_W + FNET_W:, :], preferred_element_type=f32)
    for h in range(GDN_HEADS):
        sl = slice(h * GDN_DK, (h + 1) * GDN_DK)
        o = of_ref[0, :, sl] + ob_ref[0, :, sl]
        o = o * lax.rsqrt(jnp.mean(o * o, axis=-1, keepdims=True) + EPS) * gnw_ref[...]
        y = (o * _silu(z_ref[0, :, sl])).astype(bf16)
        acc = acc + jnp.dot(y, w_ref[sl, :], preferred_element_type=f32)
    o_ref[0] = x_ref[0] + gt_ref[0] * acc


def _mix_out(o_f, o_b, z, y_f, y_c, x, gate, gdn_norm, w_out_b):
    b, l, d = x.shape
    tm = min(ROW_TILE, l)
    tok = lambda width: pl.BlockSpec((1, tm, width), lambda i, j: (i, j, 0))
    return pl.pallas_call(
        _mix_out_kernel,
        grid=(b, l // tm),
        in_specs=[tok(GDN_W), tok(GDN_W), tok(GDN_W), tok(FNET_W), tok(CONV_W), tok(d),
                  pl.BlockSpec((1, 1, d), lambda i, j: (i, 0, 0)),
                  pl.BlockSpec((1, GDN_DK), lambda i, j: (0, 0)),
                  pl.BlockSpec((d, d), lambda i, j: (0, 0))],
        out_specs=tok(d),
        out_shape=jax.ShapeDtypeStruct((b, l, d), f32),
        compiler_params=_params("parallel", "parallel"),
        name="mix_out_proj",
    )(o_f, o_b, z, y_f, y_c, x, gate, gdn_norm.reshape(1, GDN_DK), w_out_b)


def _peer_fold_kernel(keys_ref, wq_ref, o_ref):
    o_ref[...] = lax.dot_general(keys_ref[0], wq_ref[...], (((1,), (1,)), ((), ())),
                                 preferred_element_type=f32, precision=lax.Precision.HIGHEST).astype(bf16)


def _peer_fold_keys(peer_wq, peer_keys):
    d, nq = peer_wq.shape
    nhp = nq // PEER_DH
    return pl.pallas_call(
        _peer_fold_kernel,
        grid=(nhp,),
        in_specs=[pl.BlockSpec((1, PEER_NKEYS, PEER_DH), lambda i: (i, 0, 0)),
                  pl.BlockSpec((d, PEER_DH), lambda i: (0, i))],
        out_specs=pl.BlockSpec((PEER_NKEYS, d), lambda i: (i, 0)),
        out_shape=jax.ShapeDtypeStruct((nhp * PEER_NKEYS, d), bf16),
        compiler_params=_params("parallel"),
        name="peer_fold_keys",
    )(peer_keys.reshape(nhp, PEER_NKEYS, PEER_DH), peer_wq)


def _peer_query_kernel(x_ref, nw_ref, sc_ref, sh_ref, wk_ref, h_ref, s_ref):
    ht = _ada_norm(x_ref[0], nw_ref[...], sc_ref[0], sh_ref[0]).T.astype(bf16)
    h_ref[0] = ht
    s = jnp.dot(wk_ref[...], ht, preferred_element_type=f32)
    s_ref[0] = s.reshape(s_ref.shape[1:])


def _peer_query(x, nw, sc, sh, wk_b):
    b, l, d = x.shape
    tm = min(PEER_QUERY_TILE, l)
    nhp = 2 * PEER_HEADS
    vec = pl.BlockSpec((1, 1, d), lambda i, j: (i, 0, 0))
    return pl.pallas_call(
        _peer_query_kernel,
        grid=(b, l // tm),
        in_specs=[pl.BlockSpec((1, tm, d), lambda i, j: (i, j, 0)),
                  pl.BlockSpec((1, d), lambda i, j: (0, 0)), vec, vec,
                  pl.BlockSpec((nhp * PEER_NKEYS, d), lambda i, j: (0, 0))],
        out_specs=[pl.BlockSpec((1, d, tm), lambda i, j: (i, 0, j)),
                   pl.BlockSpec((1, nhp, PEER_NKEYS, tm), lambda i, j: (i, 0, 0, j))],
        out_shape=[jax.ShapeDtypeStruct((b, d, l), bf16),
                   jax.ShapeDtypeStruct((b, nhp, PEER_NKEYS, l), f32)],
        compiler_params=_params("parallel", "parallel"),
        name="peer_query",
    )(x, nw.reshape(1, d), sc, sh, wk_b)


def _batcher_pairs(n):
    pairs = []

    def merge(lo, cnt, r):
        m = 2 * r
        if m < cnt:
            merge(lo, cnt, m)
            merge(lo + r, cnt, m)
            pairs.extend((i, i + r) for i in range(lo + r, lo + cnt - r, m))
        else:
            pairs.append((lo, lo + r))

    def sort(lo, cnt):
        if cnt > 1:
            sort(lo, cnt // 2)
            sort(lo + cnt // 2, cnt // 2)
            merge(lo, cnt, 1)

    sort(0, n)
    return pairs


def _bitonic_pairs(n):
    pairs, s = [], n // 2
    while s:
        pairs.extend((i, i + s) for i in range(n) if not i & s)
        s //= 2
    return pairs


SORT16 = _batcher_pairs(PEER_TOPK)
N_CAND = 9
SORT_CAND = [(i, j) for i, j in SORT16 if j < N_CAND]
MERGE16 = _bitonic_pairs(PEER_TOPK)


def _exchange(rows, pairs):
    for i, j in pairs:
        hi, lo = jnp.maximum(rows[i], rows[j]), jnp.minimum(rows[i], rows[j])
        rows[i], rows[j] = hi, lo
    return rows


def _merge_top16(rows, shift):
    other = [pltpu.roll(r, shift, axis=0) for r in rows]
    return _exchange([jnp.maximum(rows[i], other[PEER_TOPK - 1 - i]) for i in range(PEER_TOPK)], MERGE16)


def _sorted_top16(blocks):
    rows = _exchange(list(blocks), SORT16)
    for shift in (4, 2, 1):
        rows = _merge_top16(rows, shift)
    return rows


def _all_sublanes(op, x):
    for shift in (4, 2, 1):
        x = op(x, pltpu.roll(x, shift, axis=0))
    return x


def _kth_largest16(cands):
    rows = _exchange(list(cands), SORT_CAND)
    other = [pltpu.roll(r, 4, axis=0) for r in rows]
    pad = PEER_TOPK - N_CAND
    z = ([rows[i] for i in range(pad)]
         + [jnp.maximum(rows[i], other[PEER_TOPK - 1 - i]) for i in range(pad, N_CAND)]
         + [other[PEER_TOPK - 1 - i] for i in range(N_CAND, PEER_TOPK)])
    z = _merge_top16(_exchange(z, MERGE16), 2)
    other = [pltpu.roll(r, 1, axis=0) for r in z]
    top = [jnp.maximum(z[i], other[PEER_TOPK - 1 - i]) for i in range(PEER_TOPK)]
    return functools.reduce(jnp.minimum, top)


def _candidates(av_rep, av_lo, av_hi, bv_rep, bv_lo, bv_hi, upper):
    return [av_rep[0] * bv_lo, av_rep[0] * bv_hi, av_rep[1] * bv_lo, av_rep[2] * bv_lo, av_rep[3] * bv_lo,
            jnp.where(upper, av_lo * bv_rep[0], 0.0), jnp.where(upper, av_lo * bv_rep[1], 0.0),
            jnp.where(upper, av_lo * bv_rep[2], 0.0), av_hi * bv_rep[0]]


def _peer_topk_kernel(s_ref, a_ref, b_ref, th_ref, sv1_ref, sv2_ref):
    tt = s_ref.shape[-1]
    nblk = PEER_NKEYS // SUBLANES
    half = SUBLANES
    upper = lax.broadcasted_iota(jnp.int32, (SUBLANES, tt), 0) >= SUBLANES // 2

    def head(h, carry):
        blk = lambda p, k: s_ref[0, 2 * h + p, k * SUBLANES:(k + 1) * SUBLANES, :]
        t1 = _sorted_top16([blk(0, k) for k in range(nblk)])
        t2 = _sorted_top16([blk(1, k) for k in range(nblk)])
        for r in range(PEER_TOPK):
            sv1_ref[r:r + 1, :] = t1[r][0:1, :]
            sv2_ref[r:r + 1, :] = t2[r][0:1, :]
        m1, m2 = t1[0], t2[0]
        av_rep = [jnp.exp(t1[r] - m1) for r in range(4)]
        bv_rep = [jnp.exp(t2[r] - m2) for r in range(3)]
        av_lo, av_hi = jnp.exp(sv1_ref[0:half, :] - m1), jnp.exp(sv1_ref[half:, :] - m1)
        bv_lo, bv_hi = jnp.exp(sv2_ref[0:half, :] - m2), jnp.exp(sv2_ref[half:, :] - m2)
        cand = _candidates(av_rep, av_lo, av_hi, bv_rep, bv_lo, bv_hi, upper)
        theta = _kth_largest16(cand)
        sel = [c >= theta for c in cand]
        zsum = _all_sublanes(jnp.add, sum(jnp.where(m, c, 0.0) for m, c in zip(sel, cand)))
        rz = GELU_HALF / zsum
        scaled = _candidates([a * rz for a in av_rep], av_lo * rz, av_hi * rz, bv_rep, bv_lo, bv_hi, upper)
        lowest = functools.reduce(jnp.minimum, [jnp.where(m, c, jnp.inf) for m, c in zip(sel, scaled)])
        th_ref[0, h] = _all_sublanes(jnp.minimum, lowest)[0:1, :]
        for k in range(nblk):
            rows = slice(k * SUBLANES, (k + 1) * SUBLANES)
            x1, x2 = blk(0, k), blk(1, k)
            a_ref[0, h, rows, :] = jnp.where(x1 >= t1[PEER_TOPK - 1], jnp.exp(x1 - m1) * rz, 0.0)
            b_ref[0, h, rows, :] = jnp.where(x2 >= t2[PEER_TOPK - 1], jnp.exp(x2 - m2), 0.0)
        return carry

    lax.fori_loop(0, PEER_HEADS, head, 0)


def _peer_topk(scores):
    b, nhp, nk, l = scores.shape
    tt = min(PEER_TOPK_TILE, l)
    ab_spec = pl.BlockSpec((1, PEER_HEADS, nk, tt), lambda i, j: (i, 0, 0, j))
    return pl.pallas_call(
        _peer_topk_kernel,
        grid=(b, l // tt),
        in_specs=[pl.BlockSpec((1, nhp, nk, tt), lambda i, j: (i, 0, 0, j))],
        out_specs=[ab_spec, ab_spec, pl.BlockSpec((1, PEER_HEADS, 1, tt), lambda i, j: (i, 0, 0, j))],
        out_shape=[jax.ShapeDtypeStruct((b, PEER_HEADS, nk, l), f32)] * 2
                  + [jax.ShapeDtypeStruct((b, PEER_HEADS, 1, l), f32)],
        scratch_shapes=[pltpu.VMEM((PEER_TOPK, tt), f32), pltpu.VMEM((PEER_TOPK, tt), f32)],
        compiler_params=_params("parallel", "parallel"),
        name="peer_topk",
    )(scores)


def _gelu_tanh_doubled(x):
    inner = x * (GELU_C1 + (GELU_C1 * 0.044715) * (x * x))
    return x + x * jnp.tanh(inner)


def _peer_dense_kernel(h_ref, a_ref, b_ref, th_ref, u_ref, un_ref, vt_ref, vp_ref, x_ref, gt_ref, fw_ref, o_ref,
                       acc_ref, sc_ref, sn_ref, s_ref, g_ref, gp_ref, *, final_norm):
    e = pl.program_id(2)
    tt = h_ref.shape[2]
    sub = PEER_SUB_TILE
    te = u_ref.shape[0]
    half = te // 2
    n_sub = te // sub
    keys_per_sub = sub // PEER_NKEYS
    n_first = te // PEER_NKEYS
    lane_chunks = [slice(lc * PEER_LANE_CHUNK, (lc + 1) * PEER_LANE_CHUNK) for lc in range(tt // PEER_LANE_CHUNK)]
    proj_chunks = [slice(lc * PEER_PROJ_CHUNK, (lc + 1) * PEER_PROJ_CHUNK) for lc in range(tt // PEER_PROJ_CHUNK)]
    assert n_sub * keys_per_sub == n_first and n_first >= 4 * len(proj_chunks)

    def scores(u_rows):
        return jnp.dot(u_rows, h_ref[0], preferred_element_type=f32)

    def score_piece(k):
        def run():
            if k < n_sub:
                s_ref[(k - 1) * sub:k * sub, :] = scores(u_ref[k * sub:(k + 1) * sub, :])
            else:
                sn_ref[...] = scores(un_ref[...])
        return run

    def project_piece(v_ref, src_ref, lanes):
        def run():
            acc_ref[:, lanes] += jnp.dot(v_ref[:, 0:half], src_ref[:, lanes], preferred_element_type=f32)
        return run

    pieces = {keys_per_sub * (k - 1): score_piece(k) for k in range(1, n_sub + 1)}
    for j, lanes in enumerate(proj_chunks):
        pieces[1 + keys_per_sub * j] = project_piece(vp_ref, gp_ref, lanes)
        pieces[n_first // 2 + 1 + keys_per_sub * j] = project_piece(vt_ref, g_ref, lanes)

    @pl.when(e == 0)
    def _():
        acc_ref[...] = jnp.zeros_like(acc_ref)
        gp_ref[...] = jnp.zeros_like(gp_ref)
        sc_ref[...] = scores(u_ref[0:sub, :])

    for k in range(n_sub):
        dst, base = (g_ref, k * sub) if k * sub < half else (gp_ref, k * sub - half)
        for r in range(keys_per_sub):
            ii = k * keys_per_sub + r
            if ii in pieces:
                pieces[ii]()
            for jb in range(PEER_NKEYS // PEER_ROW_CHUNK):
                second = slice(jb * PEER_ROW_CHUNK, (jb + 1) * PEER_ROW_CHUNK)
                off = r * PEER_NKEYS + jb * PEER_ROW_CHUNK
                for lanes in lane_chunks:
                    wsum = jnp.zeros((PEER_ROW_CHUNK, PEER_LANE_CHUNK), f32)
                    for h in range(PEER_HEADS):
                        p = a_ref[0, h, ii:ii + 1, lanes] * b_ref[0, h, second, lanes]
                        wsum = wsum + jnp.where(p >= th_ref[0, h, :, lanes], p, 0.0)
                    if k == 0:
                        s = sc_ref[off:off + PEER_ROW_CHUNK, lanes]
                    else:
                        s = s_ref[(k - 1) * sub + off:(k - 1) * sub + off + PEER_ROW_CHUNK, lanes]
                    dst[base + off:base + off + PEER_ROW_CHUNK, lanes] = (_gelu_tanh_doubled(s) * wsum).astype(bf16)
    sc_ref[...] = sn_ref[...]

    @pl.when(e == pl.num_programs(2) - 1)
    def _():
        acc = acc_ref[...] + jnp.dot(vt_ref[:, half:], gp_ref[...], preferred_element_type=f32)
        y = x_ref[0] + gt_ref[0] * acc.T
        if final_norm:
            y = y * lax.rsqrt(jnp.mean(y * y, axis=-1, keepdims=True) + EPS) * fw_ref[...]
        o_ref[0] = y


def _peer_dense(hb, a, bsel, theta, u_b, vt_b, x, gate, final_w, final_norm):
    b, l, d = x.shape
    ne = u_b.shape[0]
    tt = min(PEER_TOKEN_TILE, l)
    te = PEER_EXPERT_TILE
    sub = PEER_SUB_TILE
    nfirst = te // PEER_NKEYS
    n_sub = te // sub
    last_sub = ne // sub - 1
    return pl.pallas_call(
        functools.partial(_peer_dense_kernel, final_norm=final_norm),
        grid=(b, l // tt, ne // te),
        in_specs=[pl.BlockSpec((1, d, tt), lambda i, j, e: (i, 0, j)),
                  pl.BlockSpec((1, PEER_HEADS, nfirst, tt), lambda i, j, e: (i, 0, e, j)),
                  pl.BlockSpec((1, PEER_HEADS, PEER_NKEYS, tt), lambda i, j, e: (i, 0, 0, j)),
                  pl.BlockSpec((1, PEER_HEADS, 1, tt), lambda i, j, e: (i, 0, 0, j)),
                  pl.BlockSpec((te, d), lambda i, j, e: (e, 0)),
                  pl.BlockSpec((sub, d), lambda i, j, e: (jnp.minimum((e + 1) * n_sub, last_sub), 0)),
                  pl.BlockSpec((d, te), lambda i, j, e: (0, e)),
                  pl.BlockSpec((d, te // 2), lambda i, j, e: (0, jnp.maximum(2 * e - 1, 0))),
                  pl.BlockSpec((1, tt, d), lambda i, j, e: (i, j, 0)),
                  pl.BlockSpec((1, 1, d), lambda i, j, e: (i, 0, 0)),
                  pl.BlockSpec((1, d), lambda i, j, e: (0, 0))],
        out_specs=pl.BlockSpec((1, tt, d), lambda i, j, e: (i, j, 0)),
        out_shape=jax.ShapeDtypeStruct((b, l, d), f32),
        scratch_shapes=[pltpu.VMEM((d, tt), f32), pltpu.VMEM((sub, tt), f32), pltpu.VMEM((sub, tt), f32),
                        pltpu.VMEM((te - sub, tt), f32),
                        pltpu.VMEM((te // 2, tt), bf16), pltpu.VMEM((te // 2, tt), bf16)],
        compiler_params=_params("parallel", "parallel", "arbitrary"),
        name="peer_dense",
    )(hb, a, bsel, theta, u_b, u_b, vt_b, vt_b, x, gate, final_w.reshape(1, d))


def _gdn(qkv, ab, s0, lw):
    q, k, v, gb = _gdn_prep(qkv, ab, lw["conv_qkv"], lw["a_log"], lw["dt_bias"])
    return _gdn_recur(s0, *_gdn_intra(q, k, v, gb), gb)


def _mixer(x, proj, o_f, o_b, gate, row_len, lw):
    _, z, f, cgl, _ = proj
    y_f = _fnet(f)
    y_c = _conv_module(cgl, lw["dw_w"], lw["dw_b"], lw["gn_w"], lw["gn_b"], row_len)
    return _mix_out(o_f, o_b, z, y_f, y_c, x, gate, lw["gdn_norm"], lw["w_out"])


def _peer(x, sc, sh, gate, lw, final_w, final_norm=False):
    hb, scores = _peer_query(x, lw["norm2"], sc, sh, lw["peer_wk"])
    a, bsel, theta = _peer_topk(scores)
    return _peer_dense(hb, a, bsel, theta, lw["peer_u"], lw["peer_vt"], x, gate, final_w, final_norm)


def kernel(x, c, ctx, c_ctx, w_mod, b_mod, norm1, norm2, w_in, conv_qkv, a_log, dt_bias, gdn_norm, dw_w, dw_b,
           gn_w, gn_b, w_out, peer_wq, peer_keys, peer_u, peer_v, final_norm):
    bsz, _, d = x.shape
    depth = w_mod.shape[0]
    cc = jnp.zeros((SUBLANES, d), f32).at[:bsz].set(c).at[bsz].set(c_ctx)
    mods = _modulation(cc, w_mod, b_mod)
    xc = ctx
    for i in range(depth):
        last = i == depth - 1
        lw = {
            "norm2": norm2[i], "conv_qkv": conv_qkv[i], "a_log": a_log[i], "dt_bias": dt_bias[i],
            "gdn_norm": gdn_norm[i], "dw_w": dw_w[i], "dw_b": dw_b[i], "gn_w": gn_w[i], "gn_b": gn_b[i],
            "w_out": w_out[i].astype(bf16), "peer_wk": _peer_fold_keys(peer_wq[i], peer_keys[i]),
            "peer_u": peer_u[i].astype(bf16), "peer_vt": peer_v[i].T.astype(bf16),
        }
        w_in_p = _pack_w_in(w_in[i])
        mod = mods[i, :bsz].reshape(bsz, 6, 1, d)
        mod_c = jnp.broadcast_to(mods[i, bsz].reshape(1, 6, 1, d), (bsz, 6, 1, d))
        sh1, sc1, gt1, sh2, sc2, gt2 = (mod[:, j] for j in range(6))
        csh1, csc1, cgt1, csh2, csc2, cgt2 = (mod_c[:, j] for j in range(6))

        pc = _norm_in(xc, norm1[i], csc1, csh1, w_in_p)
        p = _norm_in(x, norm1[i], sc1, sh1, w_in_p)
        zero = jnp.zeros((2 * bsz * GDN_HEADS, GDN_DK, GDN_DK), f32)
        oc_f, oc_b, s_ctx = _gdn(pc[0], pc[4], zero, lw)
        o_f, o_b, _ = _gdn(p[0], p[4], s_ctx, lw)
        x = _mixer(x, p, o_f, o_b, gt1, GRID_W, lw)
        x = _peer(x, sc2, sh2, gt2, lw, final_norm, final_norm=last)
        if not last:
            xc = _mixer(xc, pc, oc_f, oc_b, cgt1, xc.shape[1], lw)
            xc = _peer(xc, csc2, csh2, cgt2, lw, final_norm)
    return x
```

```python
import functools
import math

import numpy as np
import jax
import jax.numpy as jnp
from jax import lax
from jax.experimental import pallas as pl
from jax.experimental.pallas import tpu as pltpu

f32 = jnp.float32
bf16 = jnp.bfloat16

D_MODEL = 1024
DEPTH = 2
GRID_W = 64
GDN_DK = 128
GDN_HEADS = 4
GDN_W = 512
FNET_W = 256
FNET_GROUPS = 4
FNET_GDIM = 64
CONV_W = 256
CONV_GROUPS = 4
CONV_K = 31
QKV_OFF, Z_OFF, A_OFF, B_OFF, F_OFF, C_OFF, IN_W = 0, 1536, 2048, 2056, 2064, 2320, 2832
PEER_HEADS = 8
PEER_NKEYS = 128
PEER_TOPK = 16
PEER_DH = 128
EPS = 1e-6

LANES = 128
SUBLANES = 8
VMEM_LIMIT_BYTES = 56 * 2**20

GDN_CHUNK = 128
GDN_BLOCK = 256
GDN_SUB = 16
ROW_TILE = 256
CONV_TILE = 512
FNET_POS_TILE = SUBLANES
FNET_SHORT_MAX = 512
PEER_QUERY_TILE = 512
PEER_TOKEN_TILE = 512
PEER_EXPERT_TILE = 2048
PEER_SUB_TILE = 256
PEER_TOPK_TILE = 256
PEER_LANE_CHUNK = 128
PEER_ROW_CHUNK = 64
PEER_PROJ_CHUNK = 256
GELU_HALF = 0.5
GELU_C1 = math.sqrt(2.0 / math.pi)


def _params(*sem):
    return pltpu.CompilerParams(dimension_semantics=sem, vmem_limit_bytes=VMEM_LIMIT_BYTES)


def _dot(a, b):
    return jnp.dot(a.astype(bf16), b.astype(bf16), preferred_element_type=f32)


def _dot_nt(a, b):
    return lax.dot_general(a.astype(bf16), b.astype(bf16), (((1,), (1,)), ((), ())),
                           preferred_element_type=f32)


def _dot_f32(a, b):
    return jnp.dot(a, b, preferred_element_type=f32, precision=lax.Precision.HIGHEST)


def _silu(x):
    return x * jax.nn.sigmoid(x)


def _mod_kernel(c_ref, w_ref, b_ref, o_ref):
    o_ref[0] = _dot(_silu(c_ref[...]), w_ref[0]) + b_ref[0]


def _modulation(cc, w_mod, b_mod):
    depth, d, n = w_mod.shape
    tn = n // 4
    return pl.pallas_call(
        _mod_kernel,
        grid=(depth, n // tn),
        in_specs=[pl.BlockSpec((SUBLANES, d), lambda i, j: (0, 0)),
                  pl.BlockSpec((1, d, tn), lambda i, j: (i, 0, j)),
                  pl.BlockSpec((1, 1, tn), lambda i, j: (i, 0, j))],
        out_specs=pl.BlockSpec((1, SUBLANES, tn), lambda i, j: (i, 0, j)),
        out_shape=jax.ShapeDtypeStruct((depth, SUBLANES, n), f32),
        compiler_params=_params("parallel", "parallel"),
        name="modulation",
    )(cc, w_mod, b_mod.reshape(depth, 1, n))


def _ada_norm(x, nw, sc, sh):
    r = lax.rsqrt(jnp.mean(x * x, axis=-1, keepdims=True) + EPS)
    return (x * r * nw) * (1.0 + sc) + sh


IN_SPLITS = (3 * GDN_W, GDN_W, FNET_W, 2 * CONV_W, LANES)


def _norm_in_kernel(x_ref, nw_ref, sc_ref, sh_ref, w_ref, *out_refs):
    hb = _ada_norm(x_ref[0], nw_ref[...], sc_ref[0], sh_ref[0]).astype(bf16)
    off = 0
    for o_ref, width in zip(out_refs, IN_SPLITS):
        o_ref[0] = jnp.dot(hb, w_ref[:, off:off + width], preferred_element_type=f32)
        off += width


def _pack_w_in(w_in):
    ab = jnp.pad(w_in[:, A_OFF:F_OFF], ((0, 0), (0, LANES - (F_OFF - A_OFF))))
    return jnp.concatenate([w_in[:, QKV_OFF:A_OFF], w_in[:, F_OFF:IN_W], ab], axis=1).astype(bf16)


def _norm_in(x, nw, sc, sh, w_packed):
    b, l, d = x.shape
    tm = min(ROW_TILE, l)
    n = w_packed.shape[1]
    vec = pl.BlockSpec((1, 1, d), lambda i, j: (i, 0, 0))
    return pl.pallas_call(
        _norm_in_kernel,
        grid=(b, l // tm),
        in_specs=[pl.BlockSpec((1, tm, d), lambda i, j: (i, j, 0)),
                  pl.BlockSpec((1, d), lambda i, j: (0, 0)),
                  vec, vec,
                  pl.BlockSpec((d, n), lambda i, j: (0, 0))],
        out_specs=[pl.BlockSpec((1, tm, w), lambda i, j: (i, j, 0)) for w in IN_SPLITS],
        out_shape=[jax.ShapeDtypeStruct((b, l, w), f32) for w in IN_SPLITS],
        compiler_params=_params("parallel", "parallel"),
        name="norm_in_proj",
    )(x, nw.reshape(1, d), sc, sh, w_packed)


def _split3(x):
    x1 = x.astype(bf16)
    r1 = x - x1.astype(f32)
    x2 = r1.astype(bf16)
    x3 = (r1 - x2.astype(f32)).astype(bf16)
    return x1, x2, x3


def _dot01(m_ref, parts):
    m = m_ref[...]
    return sum(jnp.dot(m, p, preferred_element_type=f32) for p in parts)


def _gdn_prep_kernel(prev_ref, x_ref, next_ref, ab_ref, cw_ref, alog_ref, dtb_ref,
                     tril_ref, triu_ref, same_ref, q_ref, k_ref, v_ref, gb_ref):
    j = pl.program_id(1)
    tl = x_ref.shape[1]
    first = j == 0
    last = j == pl.num_programs(1) - 1
    row = lax.broadcasted_iota(jnp.int32, (tl, LANES), 0)
    outs = (q_ref, k_ref, v_ref)
    for part in range(3):
        for h in range(GDN_HEADS):
            lo = part * GDN_W + h * GDN_DK
            sl = slice(lo, lo + GDN_DK)
            x = x_ref[0, :, sl]
            prev_row = jnp.where(first, 0.0, prev_ref[0, SUBLANES - 1:SUBLANES, sl])
            next_row = jnp.where(last, 0.0, next_ref[0, 0:1, sl])
            xp = jnp.where(row == 0, prev_row, pltpu.roll(x, 1, axis=0))
            xn = jnp.where(row == tl - 1, next_row, pltpu.roll(x, tl - 1, axis=0))
            y = _silu(cw_ref[0:1, sl] * xp + cw_ref[1:2, sl] * x + cw_ref[2:3, sl] * xn)
            if part < 2:
                y = y * lax.rsqrt(jnp.sum(y * y, axis=-1, keepdims=True) + EPS)
            if part == 0:
                y = y * (GDN_DK ** -0.5)
            outs[part][0, :, h * GDN_DK:(h + 1) * GDN_DK] = y
    ab = ab_ref[0]
    col = lax.broadcasted_iota(jnp.int32, ab.shape, 1)
    nh2 = 2 * GDN_HEADS
    sp = jnp.maximum(ab + dtb_ref[...], 0.0) + jnp.log1p(jnp.exp(-jnp.abs(ab + dtb_ref[...])))
    g = jnp.where(col < nh2, -jnp.exp(alog_ref[...]) * sp, 0.0)
    parts = _split3(g)
    g_f = _dot01(tril_ref, parts)
    g_b = _dot01(triu_ref, parts)
    tot = _dot01(same_ref, parts)
    beta = jax.nn.sigmoid(ab)
    out = jnp.where(col < GDN_HEADS, g_f, jnp.where(col < nh2, g_b, 0.0))
    out = out + jnp.where((col >= nh2) & (col < 2 * nh2), beta, 0.0)
    out = out + pltpu.roll(tot, 2 * nh2, axis=1)
    gb_ref[0] = out


def _chunk_masks(tb, chunk):
    i = np.arange(tb)[:, None]
    j = np.arange(tb)[None, :]
    same = (i // chunk) == (j // chunk)
    return same, i, j


def _gdn_prep(qkv, ab, conv_qkv, a_log, dt_bias):
    b, l, w = qkv.shape
    tl = min(GDN_BLOCK, l)
    nblk = tl // SUBLANES
    same, i, j = _chunk_masks(tl, GDN_CHUNK)
    tril = jnp.asarray(same & (i >= j), bf16)
    triu = jnp.asarray(same & (i <= j), bf16)
    samem = jnp.asarray(same, bf16)
    row = lambda v: jnp.pad(v.reshape(1, -1).astype(f32), ((0, 0), (0, LANES - v.size)))
    nlast = l // SUBLANES - 1
    const = lambda shape: pl.BlockSpec(shape, lambda bi, ji: (0,) * len(shape))
    tok = lambda width: pl.BlockSpec((1, tl, width), lambda bi, ji: (bi, ji, 0))
    return pl.pallas_call(
        _gdn_prep_kernel,
        grid=(b, l // tl),
        in_specs=[pl.BlockSpec((1, SUBLANES, w), lambda bi, ji: (bi, jnp.maximum(ji * nblk - 1, 0), 0)),
                  tok(w),
                  pl.BlockSpec((1, SUBLANES, w), lambda bi, ji: (bi, jnp.minimum((ji + 1) * nblk, nlast), 0)),
                  tok(LANES),
                  const((3, w)), const((1, LANES)), const((1, LANES)),
                  const((tl, tl)), const((tl, tl)), const((tl, tl))],
        out_specs=[tok(GDN_W), tok(GDN_W), tok(GDN_W), tok(LANES)],
        out_shape=[jax.ShapeDtypeStruct((b, l, GDN_W), f32)] * 3 + [jax.ShapeDtypeStruct((b, l, LANES), f32)],
        compiler_params=_params("parallel", "parallel"),
        name="gdn_prep",
    )(qkv, qkv, qkv, ab, conv_qkv, row(a_log), row(dt_bias), tril, triu, samem)


def _neumann_inverse(a_list, sub_mask, n_sub):
    d = [a * sub_mask for a in a_list]
    e = [a - x for a, x in zip(a_list, d)]
    n = [-x for x in d]
    p = n
    for _ in range(int(math.log2(GDN_SUB)) - 1):
        p = [_dot(x, x) for x in p]
        n = [nn + pp + _dot(nn, pp) for nn, pp in zip(n, p)]
    q = [-(ee + _dot(nn, ee)) for nn, ee in zip(n, e)]
    y, p = q, q
    for _ in range(int(math.log2(n_sub)) - 1):
        p = [_dot(x, x) for x in p]
        y = [yy + pp + _dot(yy, pp) for yy, pp in zip(y, p)]
    return [yy + nn + _dot(yy, nn) for yy, nn in zip(y, n)]


def _gdn_intra_kernel(q_ref, k_ref, v_ref, gb_ref, m_ref, u_ref, w_ref, qd_ref, qk_ref, kt_ref):
    tb = q_ref.shape[1]
    chunk = min(GDN_CHUNK, tb)
    gb = gb_ref[0]
    gbt = gb.T
    nh2 = 2 * GDN_HEADS
    chains, a_list, rhs_list = [], [], []
    for h in range(GDN_HEADS):
        sl = slice(h * GDN_DK, (h + 1) * GDN_DK)
        qh, kh, vh = q_ref[0, :, sl], k_ref[0, :, sl], v_ref[0, :, sl]
        kk = _dot_nt(kh, kh)
        qk = _dot_nt(qh, kh)
        kht = kh.T
        for d in range(2):
            c = d * GDN_HEADS + h
            g_col, g_row = gb[:, c:c + 1], gbt[c:c + 1, :]
            beta = gb[:, nh2 + c:nh2 + c + 1]
            tot_row = gbt[2 * nh2 + c:2 * nh2 + c + 1, :]
            decay = jnp.exp(jnp.minimum(g_col - g_row, 0.0))
            eg = jnp.exp(g_col)
            kb = kh * beta
            chains.append((d, sl))
            a_list.append(kk * beta * decay * m_ref[2 * d])
            rhs_list.append(jnp.concatenate([vh * beta, kb * eg], axis=1))
            qd_ref[d, 0, :, sl] = (qh * eg).astype(bf16)
            qkm = qk * decay * m_ref[2 * d + 1]
            for ci in range(tb // chunk):
                rs = slice(ci * chunk, (ci + 1) * chunk)
                qk_ref[d, 0, rs, h * chunk:(h + 1) * chunk] = qkm[rs, rs].astype(bf16)
            kt_ref[d, 0, sl, :] = (kht * jnp.exp(tot_row - g_row)).astype(bf16)
    nt_list = _neumann_inverse(a_list, m_ref[4], chunk // GDN_SUB)
    for (d, sl), nt, rhs in zip(chains, nt_list, rhs_list):
        sol = rhs + _dot(nt, rhs)
        u_ref[d, 0, :, sl] = sol[:, :GDN_DK]
        w_ref[d, 0, :, sl] = sol[:, GDN_DK:].astype(bf16)


def _gdn_intra(q, k, v, gb):
    b, l, w = q.shape
    tb = min(GDN_BLOCK, l)
    chunk = min(GDN_CHUNK, tb)
    same, i, j = _chunk_masks(tb, chunk)
    sub = (i // GDN_SUB) == (j // GDN_SUB)
    masks = jnp.asarray(np.stack([same & (i > j), same & (i >= j), same & (i < j), same & (i <= j), sub]), f32)
    tok = lambda width: pl.BlockSpec((1, tb, width), lambda bi, ji: (bi, ji, 0))
    dtok = lambda width: pl.BlockSpec((2, 1, tb, width), lambda bi, ji: (0, bi, ji, 0))
    sds = lambda width, dt: jax.ShapeDtypeStruct((2, b, l, width), dt)
    return pl.pallas_call(
        _gdn_intra_kernel,
        grid=(b, l // tb),
        in_specs=[tok(w), tok(w), tok(w), tok(LANES),
                  pl.BlockSpec((5, tb, tb), lambda bi, ji: (0, 0, 0))],
        out_specs=[dtok(w), dtok(w), dtok(w), dtok(GDN_HEADS * chunk),
                   pl.BlockSpec((2, 1, w, tb), lambda bi, ji: (0, bi, 0, ji))],
        out_shape=[sds(w, f32), sds(w, bf16), sds(w, bf16), sds(GDN_HEADS * chunk, bf16),
                   jax.ShapeDtypeStruct((2, b, w, l), bf16)],
        compiler_params=_params("parallel", "parallel"),
        name="gdn_intra",
    )(q, k, v, gb, masks)


def _gdn_recur_kernel(s0_ref, uf, wf, qf, kf, tf, gf, ub, wb, qb, kb, tb_, gbk, of_ref, ob_ref, sfin_ref, s_ref):
    n = pl.program_id(0)
    nb = uf.shape[1]
    chunk = uf.shape[2]

    @pl.when(n == 0)
    def _():
        s_ref[...] = s0_ref[...]

    dirs = ((uf, wf, qf, kf, tf, gf, of_ref), (ub, wb, qb, kb, tb_, gbk, ob_ref))
    chains = [(d, b, h) for d in range(2) for b in range(nb) for h in range(GDN_HEADS)]
    head = lambda h: slice(h * GDN_DK, (h + 1) * GDN_DK)
    sbs, vbs = [], []
    for idx, (d, b, h) in enumerate(chains):
        u_r, w_r = dirs[d][0], dirs[d][1]
        sb = s_ref[idx].astype(bf16)
        v_new = u_r[0, b, :, head(h)] - jnp.dot(w_r[0, b, :, head(h)], sb, preferred_element_type=f32)
        sbs.append(sb)
        vbs.append(v_new.astype(bf16))
    for idx, (d, b, h) in enumerate(chains):
        q_r, qk_r, o_r = dirs[d][2], dirs[d][3], dirs[d][6]
        o = jnp.dot(q_r[0, b, :, head(h)], sbs[idx], preferred_element_type=f32)
        o_r[b, :, head(h)] = o + jnp.dot(qk_r[0, b, :, h * chunk:(h + 1) * chunk], vbs[idx],
                                         preferred_element_type=f32)
    for idx, (d, b, h) in enumerate(chains):
        kt_r, g_r = dirs[d][4], dirs[d][5]
        c = 4 * GDN_HEADS + d * GDN_HEADS + h
        dec = jnp.exp(g_r[b, 0:1, c:c + 1])
        s_ref[idx] = s_ref[idx] * dec + jnp.dot(kt_r[0, b, head(h), :], vbs[idx], preferred_element_type=f32)

    @pl.when(n == pl.num_programs(0) - 1)
    def _():
        sfin_ref[...] = s_ref[...]


def _gdn_recur(s0, u, w, qd, qk, kt, gb):
    _, b, l, wd = u.shape
    chunk = min(GDN_CHUNK, l)
    nc = l // chunk
    fwd = lambda n: n
    bwd = lambda n: nc - 1 - n

    def specs(order, d):
        tok = lambda width: pl.BlockSpec((1, b, chunk, width), lambda n: (d, 0, order(n), 0))
        return [tok(wd), tok(wd), tok(wd), tok(GDN_HEADS * chunk),
                pl.BlockSpec((1, b, wd, chunk), lambda n: (d, 0, 0, order(n))),
                pl.BlockSpec((b, chunk, LANES), lambda n: (0, order(n), 0))]

    nstate = 2 * b * GDN_HEADS
    state = pl.BlockSpec((nstate, GDN_DK, GDN_DK), lambda n: (0, 0, 0))
    out_tok = lambda order: pl.BlockSpec((b, chunk, wd), lambda n: (0, order(n), 0))
    args = (u, w, qd, qk, kt, gb)
    return pl.pallas_call(
        _gdn_recur_kernel,
        grid=(nc,),
        in_specs=[state] + specs(fwd, 0) + specs(bwd, 1),
        out_specs=[out_tok(fwd), out_tok(bwd), state],
        out_shape=[jax.ShapeDtypeStruct((b, l, wd), f32)] * 2 + [jax.ShapeDtypeStruct((nstate, GDN_DK, GDN_DK), f32)],
        scratch_shapes=[pltpu.VMEM((nstate, GDN_DK, GDN_DK), f32)],
        compiler_params=_params("arbitrary"),
        name="gdn_recur",
    )(s0, *args, *args)


def _dft_mats(n):
    k = np.arange(n)
    ang = 2.0 * np.pi * ((k[:, None] * k[None, :]) % n) / n
    return np.cos(ang), np.sin(ang)


def _channel_dft():
    c, s = _dft_mats(FNET_GDIM)
    eye = np.eye(FNET_GROUPS)
    return jnp.asarray(np.kron(eye, c), f32), jnp.asarray(np.kron(eye, s), f32)


def _fft1_kernel(x_ref, c_ref, s_ref, tc_ref, ts_ref, br_ref, bi_ref):
    for j in range(x_ref.shape[2]):
        x = x_ref[0, :, j, :]
        ar = _dot_f32(c_ref[...], x)
        ai = -_dot_f32(s_ref[...], x)
        tc, ts = tc_ref[j], ts_ref[j]
        br_ref[0, j] = ar * tc + ai * ts
        bi_ref[0, j] = ai * tc - ar * ts


def _fft2_kernel(br_ref, bi_ref, c_ref, s_ref, cc_ref, sc_ref, o_ref, *, scale):
    c, s = c_ref[...], s_ref[...]
    for j in range(br_ref.shape[2]):
        br, bi = br_ref[0, :, j, :], bi_ref[0, :, j, :]
        xr = _dot_f32(c, br) + _dot_f32(s, bi)
        xi = _dot_f32(c, bi) - _dot_f32(s, br)
        o_ref[0, :, j, :] = (_dot_f32(xr, cc_ref[...]) + _dot_f32(xi, sc_ref[...])) * scale


def _fnet_long(f, n1, n2):
    b, l, w = f.shape
    p = FNET_POS_TILE
    c1, s1 = _dft_mats(n1)
    c2, s2 = _dft_mats(n2)
    ang = 2.0 * np.pi * ((np.arange(n2)[:, None] * np.arange(n1)[None, :]) % l) / l
    tc = jnp.asarray(np.repeat(np.cos(ang)[:, :, None], w, axis=2), f32)
    ts = jnp.asarray(np.repeat(np.sin(ang)[:, :, None], w, axis=2), f32)
    cc, sc = _channel_dft()
    const = lambda shape: pl.BlockSpec(shape, lambda bi, ji: (0,) * len(shape))
    inner = lambda n: pl.BlockSpec((1, n, p, w), lambda bi, ji: (bi, 0, ji, 0))
    br, bi = pl.pallas_call(
        _fft1_kernel,
        grid=(b, n2 // p),
        in_specs=[inner(n1), const((n1, n1)), const((n1, n1)),
                  pl.BlockSpec((p, n1, w), lambda bi, ji: (ji, 0, 0)),
                  pl.BlockSpec((p, n1, w), lambda bi, ji: (ji, 0, 0))],
        out_specs=[pl.BlockSpec((1, p, n1, w), lambda bi, ji: (bi, ji, 0, 0))] * 2,
        out_shape=[jax.ShapeDtypeStruct((b, n2, n1, w), f32)] * 2,
        compiler_params=_params("parallel", "parallel"),
        name="fnet_stage1",
    )(f.reshape(b, n1, n2, w), jnp.asarray(c1, f32), jnp.asarray(s1, f32), tc, ts)
    out = pl.pallas_call(
        functools.partial(_fft2_kernel, scale=1.0 / math.sqrt(l * FNET_GDIM)),
        grid=(b, n1 // p),
        in_specs=[inner(n2), inner(n2), const((n2, n2)), const((n2, n2)), const((w, w)), const((w, w))],
        out_specs=inner(n2),
        out_shape=jax.ShapeDtypeStruct((b, n2, n1, w), f32),
        compiler_params=_params("parallel", "parallel"),
        name="fnet_stage2",
    )(br, bi, jnp.asarray(c2, f32), jnp.asarray(s2, f32), cc, sc)
    return out.reshape(b, l, w)


def _fft_short_kernel(x_ref, c_ref, s_ref, cc_ref, sc_ref, o_ref, *, scale):
    x = x_ref[0]
    zr = _dot_f32(c_ref[...], x)
    zi = -_dot_f32(s_ref[...], x)
    o_ref[0] = (_dot_f32(zr, cc_ref[...]) + _dot_f32(zi, sc_ref[...])) * scale


def _fnet_short(f):
    b, l, w = f.shape
    c, s = _dft_mats(l)
    cc, sc = _channel_dft()
    const = lambda shape: pl.BlockSpec(shape, lambda bi: (0,) * len(shape))
    return pl.pallas_call(
        functools.partial(_fft_short_kernel, scale=1.0 / math.sqrt(l * FNET_GDIM)),
        grid=(b,),
        in_specs=[pl.BlockSpec((1, l, w), lambda bi: (bi, 0, 0)),
                  const((l, l)), const((l, l)), const((w, w)), const((w, w))],
        out_specs=pl.BlockSpec((1, l, w), lambda bi: (bi, 0, 0)),
        out_shape=jax.ShapeDtypeStruct((b, l, w), f32),
        compiler_params=_params("parallel"),
        name="fnet_short",
    )(f, jnp.asarray(c, f32), jnp.asarray(s, f32), cc, sc)


def _fnet(f):
    l = f.shape[1]
    n1 = 1 << (int(math.log2(l)) // 2)
    n2 = l // n1
    if l <= FNET_SHORT_MAX or n2 % FNET_POS_TILE or n1 % FNET_POS_TILE:
        return _fnet_short(f)
    return _fnet_long(f, n1, n2)


CONV_HALO = 16


def _conv_kernel(gl_ref, dw_ref, db_ref, gw_ref, gb_ref, avg_ref, o_ref, pad_ref, sh_ref, *, row_len):
    tl = gl_ref.shape[1]
    nr = tl // row_len
    y = gl_ref[0, :, :CONV_W] * jax.nn.sigmoid(gl_ref[0, :, CONV_W:])
    zeros = jnp.zeros((nr, CONV_HALO, CONV_W), f32)
    pad_ref[:, 0:CONV_HALO, :] = zeros
    pad_ref[:, CONV_HALO + row_len:, :] = zeros
    pad_ref[:, CONV_HALO:CONV_HALO + row_len, :] = y.reshape(nr, row_len, CONV_W)
    span = sh_ref.shape[2]
    for part in range(SUBLANES):
        sh_ref[part] = pad_ref[:, part:part + span, :]
    first = CONV_HALO - CONV_K // 2
    acc = jnp.zeros((nr, row_len, CONV_W), f32)
    for k in range(CONV_K):
        whole, part = divmod(first + k, SUBLANES)
        acc = acc + dw_ref[k:k + 1, :].reshape(1, 1, CONV_W) * sh_ref[part, :, whole * SUBLANES:whole * SUBLANES + row_len, :]
    yc = acc.reshape(tl, CONV_W) + db_ref[...]
    group_mean = lambda v: sum(jnp.dot(p, avg_ref[...], preferred_element_type=f32) for p in _split3(v))
    mu = group_mean(yc)
    cen = yc - mu
    var = group_mean(cen * cen)
    yn = cen * lax.rsqrt(var + EPS) * gw_ref[...] + gb_ref[...]
    o_ref[0] = _silu(yn)


def _conv_module(gl, dw_w, dw_b, gn_w, gn_b, row_len):
    b, l, w2 = gl.shape
    tl = max(row_len, min(CONV_TILE, l))
    gd = CONV_W // CONV_GROUPS
    assert gd & (gd - 1) == 0, "1/group_size must be exact in bf16"
    avg = jnp.asarray(np.kron(np.eye(CONV_GROUPS), np.full((gd, gd), 1.0 / gd)), bf16)
    const = lambda shape: pl.BlockSpec(shape, lambda bi, ji: (0,) * len(shape))
    vec = lambda v: v.reshape(1, CONV_W)
    return pl.pallas_call(
        functools.partial(_conv_kernel, row_len=row_len),
        grid=(b, l // tl),
        in_specs=[pl.BlockSpec((1, tl, w2), lambda bi, ji: (bi, ji, 0)),
                  const((CONV_K, CONV_W)), const((1, CONV_W)), const((1, CONV_W)), const((1, CONV_W)),
                  const((CONV_W, CONV_W))],
        out_specs=pl.BlockSpec((1, tl, CONV_W), lambda bi, ji: (bi, ji, 0)),
        out_shape=jax.ShapeDtypeStruct((b, l, CONV_W), f32),
        scratch_shapes=[pltpu.VMEM((tl // row_len, row_len + 2 * CONV_HALO, CONV_W), f32),
                        pltpu.VMEM((SUBLANES, tl // row_len, row_len + 2 * CONV_HALO - SUBLANES, CONV_W), f32)],
        compiler_params=_params("parallel", "parallel"),
        name="conv_module",
    )(gl, dw_w, vec(dw_b), vec(gn_w), vec(gn_b), avg)


def _mix_out_kernel(of_ref, ob_ref, z_ref, yf_ref, yc_ref, x_ref, gt_ref, gnw_ref, w_ref, o_ref):
    acc = jnp.dot(yf_ref[0].astype(bf16), w_ref[GDN_W:GDN_W + FNET_W, :], preferred_element_type=f32)
    acc = acc + jnp.dot(yc_ref[0].astype(bf16), w_ref[GDN_W + FNET_W:, :], preferred_element_type=f32)
    for h in range(GDN_HEADS):
        sl = slice(h * GDN_DK, (h + 1) * GDN_DK)
        o = of_ref[0, :, sl] + ob_ref[0, :, sl]
        o = o * lax.rsqrt(jnp.mean(o * o, axis=-1, keepdims=True) + EPS) * gnw_ref[...]
        y = (o * _silu(z_ref[0, :, sl])).astype(bf16)
        acc = acc + jnp.dot(y, w_ref[sl, :], preferred_element_type=f32)
    o_ref[0] = x_ref[0] + gt_ref[0] * acc


def _mix_out(o_f, o_b, z, y_f, y_c, x, gate, gdn_norm, w_out_b):
    b, l, d = x.shape
    tm = min(ROW_TILE, l)
    tok = lambda width: pl.BlockSpec((1, tm, width), lambda i, j: (i, j, 0))
    return pl.pallas_call(
        _mix_out_kernel,
        grid=(b, l // tm),
        in_specs=[tok(GDN_W), tok(GDN_W), tok(GDN_W), tok(FNET_W), tok(CONV_W), tok(d),
                  pl.BlockSpec((1, 1, d), lambda i, j: (i, 0, 0)),
                  pl.BlockSpec((1, GDN_DK), lambda i, j: (0, 0)),
                  pl.BlockSpec((d, d), lambda i, j: (0, 0))],
        out_specs=tok(d),
        out_shape=jax.ShapeDtypeStruct((b, l, d), f32),
        compiler_params=_params("parallel", "parallel"),
        name="mix_out_proj",
    )(o_f, o_b, z, y_f, y_c, x, gate, gdn_norm.reshape(1, GDN_DK), w_out_b)


def _peer_fold_kernel(keys_ref, wq_ref, o_ref):
    o_ref[...] = lax.dot_general(keys_ref[0], wq_ref[...], (((1,), (1,)), ((), ())),
                                 preferred_element_type=f32, precision=lax.Precision.HIGHEST).astype(bf16)


def _peer_fold_keys(peer_wq, peer_keys):
    d, nq = peer_wq.shape
    nhp = nq // PEER_DH
    return pl.pallas_call(
        _peer_fold_kernel,
        grid=(nhp,),
        in_specs=[pl.BlockSpec((1, PEER_NKEYS, PEER_DH), lambda i: (i, 0, 0)),
                  pl.BlockSpec((d, PEER_DH), lambda i: (0, i))],
        out_specs=pl.BlockSpec((PEER_NKEYS, d), lambda i: (i, 0)),
        out_shape=jax.ShapeDtypeStruct((nhp * PEER_NKEYS, d), bf16),
        compiler_params=_params("parallel"),
        name="peer_fold_keys",
    )(peer_keys.reshape(nhp, PEER_NKEYS, PEER_DH), peer_wq)


def _peer_query_kernel(x_ref, nw_ref, sc_ref, sh_ref, wk_ref, h_ref, s_ref):
    ht = _ada_norm(x_ref[0], nw_ref[...], sc_ref[0], sh_ref[0]).T.astype(bf16)
    h_ref[0] = ht
    s = jnp.dot(wk_ref[...], ht, preferred_element_type=f32)
    s_ref[0] = s.reshape(s_ref.shape[1:])


def _peer_query(x, nw, sc, sh, wk_b):
    b, l, d = x.shape
    tm = min(PEER_QUERY_TILE, l)
    nhp = 2 * PEER_HEADS
    vec = pl.BlockSpec((1, 1, d), lambda i, j: (i, 0, 0))
    return pl.pallas_call(
        _peer_query_kernel,
        grid=(b, l // tm),
        in_specs=[pl.BlockSpec((1, tm, d), lambda i, j: (i, j, 0)),
                  pl.BlockSpec((1, d), lambda i, j: (0, 0)), vec, vec,
                  pl.BlockSpec((nhp * PEER_NKEYS, d), lambda i, j: (0, 0))],
        out_specs=[pl.BlockSpec((1, d, tm), lambda i, j: (i, 0, j)),
                   pl.BlockSpec((1, nhp, PEER_NKEYS, tm), lambda i, j: (i, 0, 0, j))],
        out_shape=[jax.ShapeDtypeStruct((b, d, l), bf16),
                   jax.ShapeDtypeStruct((b, nhp, PEER_NKEYS, l), f32)],
        compiler_params=_params("parallel", "parallel"),
        name="peer_query",
    )(x, nw.reshape(1, d), sc, sh, wk_b)


def _batcher_pairs(n):
    pairs = []

    def merge(lo, cnt, r):
        m = 2 * r
        if m < cnt:
            merge(lo, cnt, m)
            merge(lo + r, cnt, m)
            pairs.extend((i, i + r) for i in range(lo + r, lo + cnt - r, m))
        else:
            pairs.append((lo, lo + r))

    def sort(lo, cnt):
        if cnt > 1:
            sort(lo, cnt // 2)
            sort(lo + cnt // 2, cnt // 2)
            merge(lo, cnt, 1)

    sort(0, n)
    return pairs


def _bitonic_pairs(n):
    pairs, s = [], n // 2
    while s:
        pairs.extend((i, i + s) for i in range(n) if not i & s)
        s //= 2
    return pairs


SORT16 = _batcher_pairs(PEER_TOPK)
N_CAND = 9
SORT_CAND = [(i, j) for i, j in SORT16 if j < N_CAND]
MERGE16 = _bitonic_pairs(PEER_TOPK)


def _exchange(rows, pairs):
    for i, j in pairs:
        hi, lo = jnp.maximum(rows[i], rows[j]), jnp.minimum(rows[i], rows[j])
        rows[i], rows[j] = hi, lo
    return rows


def _merge_top16(rows, shift):
    other = [pltpu.roll(r, shift, axis=0) for r in rows]
    return _exchange([jnp.maximum(rows[i], other[PEER_TOPK - 1 - i]) for i in range(PEER_TOPK)], MERGE16)


def _sorted_top16(blocks):
    rows = _exchange(list(blocks), SORT16)
    for shift in (4, 2, 1):
        rows = _merge_top16(rows, shift)
    return rows


def _all_sublanes(op, x):
    for shift in (4, 2, 1):
        x = op(x, pltpu.roll(x, shift, axis=0))
    return x


def _kth_largest16(cands):
    rows = _exchange(list(cands), SORT_CAND)
    other = [pltpu.roll(r, 4, axis=0) for r in rows]
    pad = PEER_TOPK - N_CAND
    z = ([rows[i] for i in range(pad)]
         + [jnp.maximum(rows[i], other[PEER_TOPK - 1 - i]) for i in range(pad, N_CAND)]
         + [other[PEER_TOPK - 1 - i] for i in range(N_CAND, PEER_TOPK)])
    z = _merge_top16(_exchange(z, MERGE16), 2)
    other = [pltpu.roll(r, 1, axis=0) for r in z]
    top = [jnp.maximum(z[i], other[PEER_TOPK - 1 - i]) for i in range(PEER_TOPK)]
    return functools.reduce(jnp.minimum, top)


def _candidates(av_rep, av_lo, av_hi, bv_rep, bv_lo, bv_hi, upper):
    return [av_rep[0] * bv_lo, av_rep[0] * bv_hi, av_rep[1] * bv_lo, av_rep[2] * bv_lo, av_rep[3] * bv_lo,
            jnp.where(upper, av_lo * bv_rep[0], 0.0), jnp.where(upper, av_lo * bv_rep[1], 0.0),
            jnp.where(upper, av_lo * bv_rep[2], 0.0), av_hi * bv_rep[0]]


def _peer_topk_kernel(s_ref, a_ref, b_ref, th_ref, sv1_ref, sv2_ref):
    tt = s_ref.shape[-1]
    nblk = PEER_NKEYS // SUBLANES
    half = SUBLANES
    upper = lax.broadcasted_iota(jnp.int32, (SUBLANES, tt), 0) >= SUBLANES // 2

    def head(h, carry):
        blk = lambda p, k: s_ref[0, 2 * h + p, k * SUBLANES:(k + 1) * SUBLANES, :]
        t1 = _sorted_top16([blk(0, k) for k in range(nblk)])
        t2 = _sorted_top16([blk(1, k) for k in range(nblk)])
        for r in range(PEER_TOPK):
            sv1_ref[r:r + 1, :] = t1[r][0:1, :]
            sv2_ref[r:r + 1, :] = t2[r][0:1, :]
        m1, m2 = t1[0], t2[0]
        av_rep = [jnp.exp(t1[r] - m1) for r in range(4)]
        bv_rep = [jnp.exp(t2[r] - m2) for r in range(3)]
        av_lo, av_hi = jnp.exp(sv1_ref[0:half, :] - m1), jnp.exp(sv1_ref[half:, :] - m1)
        bv_lo, bv_hi = jnp.exp(sv2_ref[0:half, :] - m2), jnp.exp(sv2_ref[half:, :] - m2)
        cand = _candidates(av_rep, av_lo, av_hi, bv_rep, bv_lo, bv_hi, upper)
        theta = _kth_largest16(cand)
        sel = [c >= theta for c in cand]
        zsum = _all_sublanes(jnp.add, sum(jnp.where(m, c, 0.0) for m, c in zip(sel, cand)))
        rz = GELU_HALF / zsum
        scaled = _candidates([a * rz for a in av_rep], av_lo * rz, av_hi * rz, bv_rep, bv_lo, bv_hi, upper)
        lowest = functools.reduce(jnp.minimum, [jnp.where(m, c, jnp.inf) for m, c in zip(sel, scaled)])
        th_ref[0, h] = _all_sublanes(jnp.minimum, lowest)[0:1, :]
        for k in range(nblk):
            rows = slice(k * SUBLANES, (k + 1) * SUBLANES)
            x1, x2 = blk(0, k), blk(1, k)
            a_ref[0, h, rows, :] = jnp.where(x1 >= t1[PEER_TOPK - 1], jnp.exp(x1 - m1) * rz, 0.0)
            b_ref[0, h, rows, :] = jnp.where(x2 >= t2[PEER_TOPK - 1], jnp.exp(x2 - m2), 0.0)
        return carry

    lax.fori_loop(0, PEER_HEADS, head, 0)


def _peer_topk(scores):
    b, nhp, nk, l = scores.shape
    tt = min(PEER_TOPK_TILE, l)
    ab_spec = pl.BlockSpec((1, PEER_HEADS, nk, tt), lambda i, j: (i, 0, 0, j))
    return pl.pallas_call(
        _peer_topk_kernel,
        grid=(b, l // tt),
        in_specs=[pl.BlockSpec((1, nhp, nk, tt), lambda i, j: (i, 0, 0, j))],
        out_specs=[ab_spec, ab_spec, pl.BlockSpec((1, PEER_HEADS, 1, tt), lambda i, j: (i, 0, 0, j))],
        out_shape=[jax.ShapeDtypeStruct((b, PEER_HEADS, nk, l), f32)] * 2
                  + [jax.ShapeDtypeStruct((b, PEER_HEADS, 1, l), f32)],
        scratch_shapes=[pltpu.VMEM((PEER_TOPK, tt), f32), pltpu.VMEM((PEER_TOPK, tt), f32)],
        compiler_params=_params("parallel", "parallel"),
        name="peer_topk",
    )(scores)


def _gelu_tanh_doubled(x):
    inner = x * (GELU_C1 + (GELU_C1 * 0.044715) * (x * x))
    return x + x * jnp.tanh(inner)


def _peer_dense_kernel(h_ref, a_ref, b_ref, th_ref, u_ref, un_ref, vt_ref, vp_ref, x_ref, gt_ref, fw_ref, o_ref,
                       acc_ref, sc_ref, sn_ref, s_ref, g_ref, gp_ref, *, final_norm):
    e = pl.program_id(2)
    tt = h_ref.shape[2]
    sub = PEER_SUB_TILE
    te = u_ref.shape[0]
    half = te // 2
    n_sub = te // sub
    keys_per_sub = sub // PEER_NKEYS
    n_first = te // PEER_NKEYS
    lane_chunks = [slice(lc * PEER_LANE_CHUNK, (lc + 1) * PEER_LANE_CHUNK) for lc in range(tt // PEER_LANE_CHUNK)]
    proj_chunks = [slice(lc * PEER_PROJ_CHUNK, (lc + 1) * PEER_PROJ_CHUNK) for lc in range(tt // PEER_PROJ_CHUNK)]
    assert n_sub * keys_per_sub == n_first and n_first >= 4 * len(proj_chunks)

    def scores(u_rows):
        return jnp.dot(u_rows, h_ref[0], preferred_element_type=f32)

    def score_piece(k):
        def run():
            if k < n_sub:
                s_ref[(k - 1) * sub:k * sub, :] = scores(u_ref[k * sub:(k + 1) * sub, :])
            else:
                sn_ref[...] = scores(un_ref[...])
        return run

    def project_piece(v_ref, src_ref, lanes):
        def run():
            acc_ref[:, lanes] += jnp.dot(v_ref[:, 0:half], src_ref[:, lanes], preferred_element_type=f32)
        return run

    pieces = {keys_per_sub * (k - 1): score_piece(k) for k in range(1, n_sub + 1)}
    for j, lanes in enumerate(proj_chunks):
        pieces[1 + keys_per_sub * j] = project_piece(vp_ref, gp_ref, lanes)
        pieces[n_first // 2 + 1 + keys_per_sub * j] = project_piece(vt_ref, g_ref, lanes)

    @pl.when(e == 0)
    def _():
        acc_ref[...] = jnp.zeros_like(acc_ref)
        gp_ref[...] = jnp.zeros_like(gp_ref)
        sc_ref[...] = scores(u_ref[0:sub, :])

    for k in range(n_sub):
        dst, base = (g_ref, k * sub) if k * sub < half else (gp_ref, k * sub - half)
        for r in range(keys_per_sub):
            ii = k * keys_per_sub + r
            if ii in pieces:
                pieces[ii]()
            for jb in range(PEER_NKEYS // PEER_ROW_CHUNK):
                second = slice(jb * PEER_ROW_CHUNK, (jb + 1) * PEER_ROW_CHUNK)
                off = r * PEER_NKEYS + jb * PEER_ROW_CHUNK
                for lanes in lane_chunks:
                    wsum = jnp.zeros((PEER_ROW_CHUNK, PEER_LANE_CHUNK), f32)
                    for h in range(PEER_HEADS):
                        p = a_ref[0, h, ii:ii + 1, lanes] * b_ref[0, h, second, lanes]
                        wsum = wsum + jnp.where(p >= th_ref[0, h, :, lanes], p, 0.0)
                    if k == 0:
                        s = sc_ref[off:off + PEER_ROW_CHUNK, lanes]
                    else:
                        s = s_ref[(k - 1) * sub + off:(k - 1) * sub + off + PEER_ROW_CHUNK, lanes]
                    dst[base + off:base + off + PEER_ROW_CHUNK, lanes] = (
                        _gelu_tanh_doubled(s.astype(bf16)) * wsum.astype(bf16))
    sc_ref[...] = sn_ref[...]

    @pl.when(e == pl.num_programs(2) - 1)
    def _():
        acc = acc_ref[...] + jnp.dot(vt_ref[:, half:], gp_ref[...], preferred_element_type=f32)
        y = x_ref[0] + gt_ref[0] * acc.T
        if final_norm:
            y = y * lax.rsqrt(jnp.mean(y * y, axis=-1, keepdims=True) + EPS) * fw_ref[...]
        o_ref[0] = y


def _peer_dense(hb, a, bsel, theta, u_b, vt_b, x, gate, final_w, final_norm):
    b, l, d = x.shape
    ne = u_b.shape[0]
    tt = min(PEER_TOKEN_TILE, l)
    te = PEER_EXPERT_TILE
    sub = PEER_SUB_TILE
    nfirst = te // PEER_NKEYS
    n_sub = te // sub
    last_sub = ne // sub - 1
    return pl.pallas_call(
        functools.partial(_peer_dense_kernel, final_norm=final_norm),
        grid=(b, l // tt, ne // te),
        in_specs=[pl.BlockSpec((1, d, tt), lambda i, j, e: (i, 0, j)),
                  pl.BlockSpec((1, PEER_HEADS, nfirst, tt), lambda i, j, e: (i, 0, e, j)),
                  pl.BlockSpec((1, PEER_HEADS, PEER_NKEYS, tt), lambda i, j, e: (i, 0, 0, j)),
                  pl.BlockSpec((1, PEER_HEADS, 1, tt), lambda i, j, e: (i, 0, 0, j)),
                  pl.BlockSpec((te, d), lambda i, j, e: (e, 0)),
                  pl.BlockSpec((sub, d), lambda i, j, e: (jnp.minimum((e + 1) * n_sub, last_sub), 0)),
                  pl.BlockSpec((d, te), lambda i, j, e: (0, e)),
                  pl.BlockSpec((d, te // 2), lambda i, j, e: (0, jnp.maximum(2 * e - 1, 0))),
                  pl.BlockSpec((1, tt, d), lambda i, j, e: (i, j, 0)),
                  pl.BlockSpec((1, 1, d), lambda i, j, e: (i, 0, 0)),
                  pl.BlockSpec((1, d), lambda i, j, e: (0, 0))],
        out_specs=pl.BlockSpec((1, tt, d), lambda i, j, e: (i, j, 0)),
        out_shape=jax.ShapeDtypeStruct((b, l, d), f32),
        scratch_shapes=[pltpu.VMEM((d, tt), f32), pltpu.VMEM((sub, tt), f32), pltpu.VMEM((sub, tt), f32),
                        pltpu.VMEM((te - sub, tt), f32),
                        pltpu.VMEM((te // 2, tt), bf16), pltpu.VMEM((te // 2, tt), bf16)],
        compiler_params=_params("parallel", "parallel", "arbitrary"),
        name="peer_dense",
    )(hb, a, bsel, theta, u_b, u_b, vt_b, vt_b, x, gate, final_w.reshape(1, d))


def _gdn(qkv, ab, s0, lw):
    q, k, v, gb = _gdn_prep(qkv, ab, lw["conv_qkv"], lw["a_log"], lw["dt_bias"])
    return _gdn_recur(s0, *_gdn_intra(q, k, v, gb), gb)


def _mixer(x, proj, o_f, o_b, gate, row_len, lw):
    _, z, f, cgl, _ = proj
    y_f = _fnet(f)
    y_c = _conv_module(cgl, lw["dw_w"], lw["dw_b"], lw["gn_w"], lw["gn_b"], row_len)
    return _mix_out(o_f, o_b, z, y_f, y_c, x, gate, lw["gdn_norm"], lw["w_out"])


def _peer(x, sc, sh, gate, lw, final_w, final_norm=False):
    hb, scores = _peer_query(x, lw["norm2"], sc, sh, lw["peer_wk"])
    a, bsel, theta = _peer_topk(scores)
    return _peer_dense(hb, a, bsel, theta, lw["peer_u"], lw["peer_vt"], x, gate, final_w, final_norm)


def kernel(x, c, ctx, c_ctx, w_mod, b_mod, norm1, norm2, w_in, conv_qkv, a_log, dt_bias, gdn_norm, dw_w, dw_b,
           gn_w, gn_b, w_out, peer_wq, peer_keys, peer_u, peer_v, final_norm):
    bsz, _, d = x.shape
    depth = w_mod.shape[0]
    cc = jnp.zeros((SUBLANES, d), f32).at[:bsz].set(c).at[bsz].set(c_ctx)
    mods = _modulation(cc, w_mod, b_mod)
    xc = ctx
    for i in range(depth):
        last = i == depth - 1
        lw = {
            "norm2": norm2[i], "conv_qkv": conv_qkv[i], "a_log": a_log[i], "dt_bias": dt_bias[i],
            "gdn_norm": gdn_norm[i], "dw_w": dw_w[i], "dw_b": dw_b[i], "gn_w": gn_w[i], "gn_b": gn_b[i],
            "w_out": w_out[i].astype(bf16), "peer_wk": _peer_fold_keys(peer_wq[i], peer_keys[i]),
            "peer_u": peer_u[i].astype(bf16), "peer_vt": peer_v[i].T.astype(bf16),
        }
        w_in_p = _pack_w_in(w_in[i])
        mod = mods[i, :bsz].reshape(bsz, 6, 1, d)
        mod_c = jnp.broadcast_to(mods[i, bsz].reshape(1, 6, 1, d), (bsz, 6, 1, d))
        sh1, sc1, gt1, sh2, sc2, gt2 = (mod[:, j] for j in range(6))
        csh1, csc1, cgt1, csh2, csc2, cgt2 = (mod_c[:, j] for j in range(6))

        pc = _norm_in(xc, norm1[i], csc1, csh1, w_in_p)
        p = _norm_in(x, norm1[i], sc1, sh1, w_in_p)
        zero = jnp.zeros((2 * bsz * GDN_HEADS, GDN_DK, GDN_DK), f32)
        oc_f, oc_b, s_ctx = _gdn(pc[0], pc[4], zero, lw)
        o_f, o_b, _ = _gdn(p[0], p[4], s_ctx, lw)
        x = _mixer(x, p, o_f, o_b, gt1, GRID_W, lw)
        x = _peer(x, sc2, sh2, gt2, lw, final_norm, final_norm=last)
        if not last:
            xc = _mixer(xc, pc, oc_f, oc_b, cgt1, xc.shape[1], lw)
            xc = _peer(xc, csc2, csh2, cgt2, lw, final_norm)
    return x
```

```python
import functools
import math

import numpy as np
import jax
import jax.numpy as jnp
from jax import lax
from jax.experimental import pallas as pl
from jax.experimental.pallas import tpu as pltpu

f32 = jnp.float32
bf16 = jnp.bfloat16

D_MODEL = 1024
DEPTH = 2
GRID_W = 64
GDN_DK = 128
GDN_HEADS = 4
GDN_W = 512
FNET_W = 256
FNET_GROUPS = 4
FNET_GDIM = 64
CONV_W = 256
CONV_GROUPS = 4
CONV_K = 31
QKV_OFF, Z_OFF, A_OFF, B_OFF, F_OFF, C_OFF, IN_W = 0, 1536, 2048, 2056, 2064, 2320, 2832
PEER_HEADS = 8
PEER_NKEYS = 128
PEER_TOPK = 16
PEER_DH = 128
EPS = 1e-6

LANES = 128
SUBLANES = 8
VMEM_LIMIT_BYTES = 56 * 2**20

GDN_CHUNK = 128
GDN_BLOCK = 256
GDN_SUB = 16
ROW_TILE = 256
CONV_TILE = 512
FNET_POS_TILE = SUBLANES
FNET_SHORT_MAX = 512
PEER_QUERY_TILE = 512
PEER_TOKEN_TILE = 512
PEER_EXPERT_TILE = 2048
PEER_SUB_TILE = 256
PEER_TOPK_TILE = 256
PEER_LANE_CHUNK = 128
PEER_ROW_CHUNK = 64
PEER_PROJ_CHUNK = 256
GELU_HALF = 0.5
GELU_C1 = math.sqrt(2.0 / math.pi)


def _params(*sem):
    return pltpu.CompilerParams(dimension_semantics=sem, vmem_limit_bytes=VMEM_LIMIT_BYTES)


def _dot(a, b):
    return jnp.dot(a.astype(bf16), b.astype(bf16), preferred_element_type=f32)


def _dot_nt(a, b):
    return lax.dot_general(a.astype(bf16), b.astype(bf16), (((1,), (1,)), ((), ())),
                           preferred_element_type=f32)


def _dot_f32(a, b):
    return jnp.dot(a, b, preferred_element_type=f32, precision=lax.Precision.HIGHEST)


def _silu(x):
    return x * jax.nn.sigmoid(x)


def _mod_kernel(c_ref, w_ref, b_ref, o_ref):
    o_ref[0] = _dot(_silu(c_ref[...]), w_ref[0]) + b_ref[0]


def _modulation(cc, w_mod, b_mod):
    depth, d, n = w_mod.shape
    tn = n // 4
    return pl.pallas_call(
        _mod_kernel,
        grid=(depth, n // tn),
        in_specs=[pl.BlockSpec((SUBLANES, d), lambda i, j: (0, 0)),
                  pl.BlockSpec((1, d, tn), lambda i, j: (i, 0, j)),
                  pl.BlockSpec((1, 1, tn), lambda i, j: (i, 0, j))],
        out_specs=pl.BlockSpec((1, SUBLANES, tn), lambda i, j: (i, 0, j)),
        out_shape=jax.ShapeDtypeStruct((depth, SUBLANES, n), f32),
        compiler_params=_params("parallel", "parallel"),
        name="modulation",
    )(cc, w_mod, b_mod.reshape(depth, 1, n))


def _ada_norm(x, nw, sc, sh):
    r = lax.rsqrt(jnp.mean(x * x, axis=-1, keepdims=True) + EPS)
    return (x * r * nw) * (1.0 + sc) + sh


IN_SPLITS = (3 * GDN_W, GDN_W, FNET_W, 2 * CONV_W, LANES)
IN_OFFSETS = tuple(int(v) for v in np.cumsum((0,) + IN_SPLITS))


def _pack_w_in(w_in):
    ab = jnp.pad(w_in[:, A_OFF:F_OFF], ((0, 0), (0, LANES - (F_OFF - A_OFF))))
    return jnp.concatenate([w_in[:, QKV_OFF:A_OFF], w_in[:, F_OFF:IN_W], ab], axis=1).astype(bf16)


def _split3(x):
    x1 = x.astype(bf16)
    r1 = x - x1.astype(f32)
    x2 = r1.astype(bf16)
    x3 = (r1 - x2.astype(f32)).astype(bf16)
    return x1, x2, x3


def _dot01(m_ref, parts):
    m = m_ref[...]
    return sum(jnp.dot(m, p, preferred_element_type=f32) for p in parts)


def _norm_in_kernel(prev_ref, x_ref, next_ref, nw_ref, sc_ref, sh_ref, w_ref, cw_ref, alog_ref, dtb_ref,
                    tril_ref, triu_ref, same_ref, q_ref, k_ref, v_ref, gb_ref, z_ref, f_ref, c_ref):
    j = pl.program_id(1)
    tl = x_ref.shape[1]
    first = j == 0
    last = j == pl.num_programs(1) - 1
    x_all = jnp.concatenate([prev_ref[0], x_ref[0], next_ref[0]], axis=0)
    h_all = _ada_norm(x_all, nw_ref[...], sc_ref[0], sh_ref[0])
    hb = h_all[SUBLANES:SUBLANES + tl].astype(bf16)
    proj = lambda i: jnp.dot(hb, w_ref[:, IN_OFFSETS[i]:IN_OFFSETS[i + 1]], preferred_element_type=f32)
    z_ref[0] = proj(1).astype(z_ref.dtype)
    f_ref[0] = proj(2)
    c_ref[0] = proj(3).astype(c_ref.dtype)
    ab = proj(4)

    qkv = jnp.dot(h_all.astype(bf16), w_ref[:, IN_OFFSETS[0]:IN_OFFSETS[1]], preferred_element_type=f32)
    row = lax.broadcasted_iota(jnp.int32, (tl, LANES), 0)
    outs = (q_ref, k_ref, v_ref)
    for part in range(3):
        for h in range(GDN_HEADS):
            lo = part * GDN_W + h * GDN_DK
            sl = slice(lo, lo + GDN_DK)
            x = qkv[SUBLANES:SUBLANES + tl, sl]
            prev_row = jnp.where(first, 0.0, qkv[SUBLANES - 1:SUBLANES, sl])
            next_row = jnp.where(last, 0.0, qkv[SUBLANES + tl:SUBLANES + tl + 1, sl])
            xp = jnp.where(row == 0, prev_row, pltpu.roll(x, 1, axis=0))
            xn = jnp.where(row == tl - 1, next_row, pltpu.roll(x, tl - 1, axis=0))
            y = _silu(cw_ref[0:1, sl] * xp + cw_ref[1:2, sl] * x + cw_ref[2:3, sl] * xn)
            if part < 2:
                y = y * lax.rsqrt(jnp.sum(y * y, axis=-1, keepdims=True) + EPS)
            if part == 0:
                y = y * (GDN_DK ** -0.5)
            outs[part][0, :, h * GDN_DK:(h + 1) * GDN_DK] = y
    col = lax.broadcasted_iota(jnp.int32, ab.shape, 1)
    nh2 = 2 * GDN_HEADS
    sp = jnp.maximum(ab + dtb_ref[...], 0.0) + jnp.log1p(jnp.exp(-jnp.abs(ab + dtb_ref[...])))
    g = jnp.where(col < nh2, -jnp.exp(alog_ref[...]) * sp, 0.0)
    parts = _split3(g)
    g_f = _dot01(tril_ref, parts)
    g_b = _dot01(triu_ref, parts)
    tot = _dot01(same_ref, parts)
    beta = jax.nn.sigmoid(ab)
    out = jnp.where(col < GDN_HEADS, g_f, jnp.where(col < nh2, g_b, 0.0))
    out = out + jnp.where((col >= nh2) & (col < 2 * nh2), beta, 0.0)
    out = out + pltpu.roll(tot, 2 * nh2, axis=1)
    gb_ref[0] = out


def _chunk_masks(tb, chunk):
    i = np.arange(tb)[:, None]
    j = np.arange(tb)[None, :]
    same = (i // chunk) == (j // chunk)
    return same, i, j


def _norm_in(x, nw, sc, sh, w_packed, conv_qkv, a_log, dt_bias):
    b, l, d = x.shape
    tl = min(GDN_BLOCK, l)
    n = w_packed.shape[1]
    nblk = tl // SUBLANES
    nlast = l // SUBLANES - 1
    same, i, j = _chunk_masks(tl, GDN_CHUNK)
    tril = jnp.asarray(same & (i >= j), bf16)
    triu = jnp.asarray(same & (i <= j), bf16)
    samem = jnp.asarray(same, bf16)
    row = lambda v: jnp.pad(v.reshape(1, -1).astype(f32), ((0, 0), (0, LANES - v.size)))
    const = lambda shape: pl.BlockSpec(shape, lambda bi, ji: (0,) * len(shape))
    vec = pl.BlockSpec((1, 1, d), lambda bi, ji: (bi, 0, 0))
    tok = lambda width: pl.BlockSpec((1, tl, width), lambda bi, ji: (bi, ji, 0))
    sds = lambda width, dt: jax.ShapeDtypeStruct((b, l, width), dt)
    return pl.pallas_call(
        _norm_in_kernel,
        grid=(b, l // tl),
        in_specs=[pl.BlockSpec((1, SUBLANES, d), lambda bi, ji: (bi, jnp.maximum(ji * nblk - 1, 0), 0)),
                  tok(d),
                  pl.BlockSpec((1, SUBLANES, d), lambda bi, ji: (bi, jnp.minimum((ji + 1) * nblk, nlast), 0)),
                  const((1, d)), vec, vec, const((d, n)),
                  const((3, 3 * GDN_W)), const((1, LANES)), const((1, LANES)),
                  const((tl, tl)), const((tl, tl)), const((tl, tl))],
        out_specs=[tok(GDN_W), tok(GDN_W), tok(GDN_W), tok(LANES), tok(GDN_W), tok(FNET_W), tok(2 * CONV_W)],
        out_shape=[sds(GDN_W, f32)] * 3 + [sds(LANES, f32), sds(GDN_W, bf16), sds(FNET_W, f32), sds(2 * CONV_W, bf16)],
        compiler_params=_params("parallel", "parallel"),
        name="norm_in_proj",
    )(x, x, x, nw.reshape(1, d), sc, sh, w_packed, conv_qkv, row(a_log), row(dt_bias), tril, triu, samem)


def _neumann_inverse(a_list, sub_mask, n_sub):
    d = [a * sub_mask for a in a_list]
    e = [a - x for a, x in zip(a_list, d)]
    n = [-x for x in d]
    p = n
    for _ in range(int(math.log2(GDN_SUB)) - 1):
        p = [_dot(x, x) for x in p]
        n = [nn + pp + _dot(nn, pp) for nn, pp in zip(n, p)]
    q = [-(ee + _dot(nn, ee)) for nn, ee in zip(n, e)]
    y, p = q, q
    for _ in range(int(math.log2(n_sub)) - 1):
        p = [_dot(x, x) for x in p]
        y = [yy + pp + _dot(yy, pp) for yy, pp in zip(y, p)]
    return [yy + nn + _dot(yy, nn) for yy, nn in zip(y, n)]


def _gdn_intra_kernel(q_ref, k_ref, v_ref, gb_ref, m_ref, u_ref, w_ref, qd_ref, qk_ref, kt_ref):
    tb = q_ref.shape[1]
    chunk = min(GDN_CHUNK, tb)
    gb = gb_ref[0]
    gbt = gb.T
    nh2 = 2 * GDN_HEADS
    chains, a_list, rhs_list = [], [], []
    for h in range(GDN_HEADS):
        sl = slice(h * GDN_DK, (h + 1) * GDN_DK)
        qh, kh, vh = q_ref[0, :, sl], k_ref[0, :, sl], v_ref[0, :, sl]
        kk = _dot_nt(kh, kh)
        qk = _dot_nt(qh, kh)
        kht = kh.T
        for d in range(2):
            c = d * GDN_HEADS + h
            g_col, g_row = gb[:, c:c + 1], gbt[c:c + 1, :]
            beta = gb[:, nh2 + c:nh2 + c + 1]
            tot_row = gbt[2 * nh2 + c:2 * nh2 + c + 1, :]
            decay = jnp.exp(jnp.minimum(g_col - g_row, 0.0))
            eg = jnp.exp(g_col)
            kb = kh * beta
            chains.append((d, sl))
            a_list.append(kk * beta * decay * m_ref[2 * d])
            rhs_list.append(jnp.concatenate([vh * beta, kb * eg], axis=1))
            qd_ref[d, 0, :, sl] = (qh * eg).astype(bf16)
            qkm = qk * decay * m_ref[2 * d + 1]
            for ci in range(tb // chunk):
                rs = slice(ci * chunk, (ci + 1) * chunk)
                qk_ref[d, 0, rs, h * chunk:(h + 1) * chunk] = qkm[rs, rs].astype(bf16)
            kt_ref[d, 0, sl, :] = (kht * jnp.exp(tot_row - g_row)).astype(bf16)
    nt_list = _neumann_inverse(a_list, m_ref[4], chunk // GDN_SUB)
    for (d, sl), nt, rhs in zip(chains, nt_list, rhs_list):
        sol = rhs + _dot(nt, rhs)
        u_ref[d, 0, :, sl] = sol[:, :GDN_DK]
        w_ref[d, 0, :, sl] = sol[:, GDN_DK:].astype(bf16)


def _gdn_intra(q, k, v, gb):
    b, l, w = q.shape
    tb = min(GDN_BLOCK, l)
    chunk = min(GDN_CHUNK, tb)
    same, i, j = _chunk_masks(tb, chunk)
    sub = (i // GDN_SUB) == (j // GDN_SUB)
    masks = jnp.asarray(np.stack([same & (i > j), same & (i >= j), same & (i < j), same & (i <= j), sub]), f32)
    tok = lambda width: pl.BlockSpec((1, tb, width), lambda bi, ji: (bi, ji, 0))
    dtok = lambda width: pl.BlockSpec((2, 1, tb, width), lambda bi, ji: (0, bi, ji, 0))
    sds = lambda width, dt: jax.ShapeDtypeStruct((2, b, l, width), dt)
    return pl.pallas_call(
        _gdn_intra_kernel,
        grid=(b, l // tb),
        in_specs=[tok(w), tok(w), tok(w), tok(LANES),
                  pl.BlockSpec((5, tb, tb), lambda bi, ji: (0, 0, 0))],
        out_specs=[dtok(w), dtok(w), dtok(w), dtok(GDN_HEADS * chunk),
                   pl.BlockSpec((2, 1, w, tb), lambda bi, ji: (0, bi, 0, ji))],
        out_shape=[sds(w, f32), sds(w, bf16), sds(w, bf16), sds(GDN_HEADS * chunk, bf16),
                   jax.ShapeDtypeStruct((2, b, w, l), bf16)],
        compiler_params=_params("parallel", "parallel"),
        name="gdn_intra",
    )(q, k, v, gb, masks)


def _gdn_recur_kernel(s0_ref, uf, wf, qf, kf, tf, gf, ub, wb, qb, kb, tb_, gbk, of_ref, ob_ref, sfin_ref, s_ref):
    n = pl.program_id(0)
    nb = uf.shape[1]
    chunk = uf.shape[2]

    @pl.when(n == 0)
    def _():
        s_ref[...] = s0_ref[...]

    dirs = ((uf, wf, qf, kf, tf, gf, of_ref), (ub, wb, qb, kb, tb_, gbk, ob_ref))
    chains = [(d, b, h) for d in range(2) for b in range(nb) for h in range(GDN_HEADS)]
    head = lambda h: slice(h * GDN_DK, (h + 1) * GDN_DK)
    sbs, vbs = [], []
    for idx, (d, b, h) in enumerate(chains):
        u_r, w_r = dirs[d][0], dirs[d][1]
        sb = s_ref[idx].astype(bf16)
        v_new = u_r[0, b, :, head(h)] - jnp.dot(w_r[0, b, :, head(h)], sb, preferred_element_type=f32)
        sbs.append(sb)
        vbs.append(v_new.astype(bf16))
    for idx, (d, b, h) in enumerate(chains):
        q_r, qk_r, o_r = dirs[d][2], dirs[d][3], dirs[d][6]
        o = jnp.dot(q_r[0, b, :, head(h)], sbs[idx], preferred_element_type=f32)
        o = o + jnp.dot(qk_r[0, b, :, h * chunk:(h + 1) * chunk], vbs[idx], preferred_element_type=f32)
        o_r[b, :, head(h)] = o.astype(o_r.dtype)
    for idx, (d, b, h) in enumerate(chains):
        kt_r, g_r = dirs[d][4], dirs[d][5]
        c = 4 * GDN_HEADS + d * GDN_HEADS + h
        dec = jnp.exp(g_r[b, 0:1, c:c + 1])
        s_ref[idx] = s_ref[idx] * dec + jnp.dot(kt_r[0, b, head(h), :], vbs[idx], preferred_element_type=f32)

    @pl.when(n == pl.num_programs(0) - 1)
    def _():
        sfin_ref[...] = s_ref[...]


def _gdn_recur(s0, u, w, qd, qk, kt, gb):
    _, b, l, wd = u.shape
    chunk = min(GDN_CHUNK, l)
    nc = l // chunk
    fwd = lambda n: n
    bwd = lambda n: nc - 1 - n

    def specs(order, d):
        tok = lambda width: pl.BlockSpec((1, b, chunk, width), lambda n: (d, 0, order(n), 0))
        return [tok(wd), tok(wd), tok(wd), tok(GDN_HEADS * chunk),
                pl.BlockSpec((1, b, wd, chunk), lambda n: (d, 0, 0, order(n))),
                pl.BlockSpec((b, chunk, LANES), lambda n: (0, order(n), 0))]

    nstate = 2 * b * GDN_HEADS
    state = pl.BlockSpec((nstate, GDN_DK, GDN_DK), lambda n: (0, 0, 0))
    out_tok = lambda order: pl.BlockSpec((b, chunk, wd), lambda n: (0, order(n), 0))
    args = (u, w, qd, qk, kt, gb)
    return pl.pallas_call(
        _gdn_recur_kernel,
        grid=(nc,),
        in_specs=[state] + specs(fwd, 0) + specs(bwd, 1),
        out_specs=[out_tok(fwd), out_tok(bwd), state],
        out_shape=[jax.ShapeDtypeStruct((b, l, wd), bf16)] * 2 + [jax.ShapeDtypeStruct((nstate, GDN_DK, GDN_DK), f32)],
        scratch_shapes=[pltpu.VMEM((nstate, GDN_DK, GDN_DK), f32)],
        compiler_params=_params("arbitrary"),
        name="gdn_recur",
    )(s0, *args, *args)


def _dft_mats(n):
    k = np.arange(n)
    ang = 2.0 * np.pi * ((k[:, None] * k[None, :]) % n) / n
    return np.cos(ang), np.sin(ang)


def _channel_dft():
    c, s = _dft_mats(FNET_GDIM)
    eye = np.eye(FNET_GROUPS)
    return jnp.asarray(np.kron(eye, c), f32), jnp.asarray(np.kron(eye, s), f32)


def _fft1_kernel(x_ref, c_ref, s_ref, tc_ref, ts_ref, br_ref, bi_ref):
    for j in range(x_ref.shape[2]):
        x = x_ref[0, :, j, :]
        ar = _dot_f32(c_ref[...], x)
        ai = -_dot_f32(s_ref[...], x)
        tc, ts = tc_ref[j], ts_ref[j]
        br_ref[0, j] = ar * tc + ai * ts
        bi_ref[0, j] = ai * tc - ar * ts


def _fft2_kernel(br_ref, bi_ref, c_ref, s_ref, cc_ref, sc_ref, o_ref, *, scale):
    c, s = c_ref[...], s_ref[...]
    for j in range(br_ref.shape[2]):
        br, bi = br_ref[0, :, j, :], bi_ref[0, :, j, :]
        xr = _dot_f32(c, br) + _dot_f32(s, bi)
        xi = _dot_f32(c, bi) - _dot_f32(s, br)
        o_ref[0, :, j, :] = (_dot_f32(xr, cc_ref[...]) + _dot_f32(xi, sc_ref[...])) * scale


def _fnet_long(f, n1, n2):
    b, l, w = f.shape
    p = FNET_POS_TILE
    c1, s1 = _dft_mats(n1)
    c2, s2 = _dft_mats(n2)
    ang = 2.0 * np.pi * ((np.arange(n2)[:, None] * np.arange(n1)[None, :]) % l) / l
    tc = jnp.asarray(np.repeat(np.cos(ang)[:, :, None], w, axis=2), f32)
    ts = jnp.asarray(np.repeat(np.sin(ang)[:, :, None], w, axis=2), f32)
    cc, sc = _channel_dft()
    const = lambda shape: pl.BlockSpec(shape, lambda bi, ji: (0,) * len(shape))
    inner = lambda n: pl.BlockSpec((1, n, p, w), lambda bi, ji: (bi, 0, ji, 0))
    br, bi = pl.pallas_call(
        _fft1_kernel,
        grid=(b, n2 // p),
        in_specs=[inner(n1), const((n1, n1)), const((n1, n1)),
                  pl.BlockSpec((p, n1, w), lambda bi, ji: (ji, 0, 0)),
                  pl.BlockSpec((p, n1, w), lambda bi, ji: (ji, 0, 0))],
        out_specs=[pl.BlockSpec((1, p, n1, w), lambda bi, ji: (bi, ji, 0, 0))] * 2,
        out_shape=[jax.ShapeDtypeStruct((b, n2, n1, w), f32)] * 2,
        compiler_params=_params("parallel", "parallel"),
        name="fnet_stage1",
    )(f.reshape(b, n1, n2, w), jnp.asarray(c1, f32), jnp.asarray(s1, f32), tc, ts)
    out = pl.pallas_call(
        functools.partial(_fft2_kernel, scale=1.0 / math.sqrt(l * FNET_GDIM)),
        grid=(b, n1 // p),
        in_specs=[inner(n2), inner(n2), const((n2, n2)), const((n2, n2)), const((w, w)), const((w, w))],
        out_specs=inner(n2),
        out_shape=jax.ShapeDtypeStruct((b, n2, n1, w), f32),
        compiler_params=_params("parallel", "parallel"),
        name="fnet_stage2",
    )(br, bi, jnp.asarray(c2, f32), jnp.asarray(s2, f32), cc, sc)
    return out.reshape(b, l, w)


def _fft_short_kernel(x_ref, c_ref, s_ref, cc_ref, sc_ref, o_ref, *, scale):
    x = x_ref[0]
    zr = _dot_f32(c_ref[...], x)
    zi = -_dot_f32(s_ref[...], x)
    o_ref[0] = (_dot_f32(zr, cc_ref[...]) + _dot_f32(zi, sc_ref[...])) * scale


def _fnet_short(f):
    b, l, w = f.shape
    c, s = _dft_mats(l)
    cc, sc = _channel_dft()
    const = lambda shape: pl.BlockSpec(shape, lambda bi: (0,) * len(shape))
    return pl.pallas_call(
        functools.partial(_fft_short_kernel, scale=1.0 / math.sqrt(l * FNET_GDIM)),
        grid=(b,),
        in_specs=[pl.BlockSpec((1, l, w), lambda bi: (bi, 0, 0)),
                  const((l, l)), const((l, l)), const((w, w)), const((w, w))],
        out_specs=pl.BlockSpec((1, l, w), lambda bi: (bi, 0, 0)),
        out_shape=jax.ShapeDtypeStruct((b, l, w), f32),
        compiler_params=_params("parallel"),
        name="fnet_short",
    )(f, jnp.asarray(c, f32), jnp.asarray(s, f32), cc, sc)


def _fnet(f):
    l = f.shape[1]
    n1 = 1 << (int(math.log2(l)) // 2)
    n2 = l // n1
    if l <= FNET_SHORT_MAX or n2 % FNET_POS_TILE or n1 % FNET_POS_TILE:
        return _fnet_short(f)
    return _fnet_long(f, n1, n2)


CONV_HALO = 16


def _conv_kernel(gl_ref, dw_ref, db_ref, gw_ref, gb_ref, avg_ref, o_ref, pad_ref, sh_ref, *, row_len):
    tl = gl_ref.shape[1]
    nr = tl // row_len
    y = gl_ref[0, :, :CONV_W].astype(f32) * jax.nn.sigmoid(gl_ref[0, :, CONV_W:].astype(f32))
    zeros = jnp.zeros((nr, CONV_HALO, CONV_W), f32)
    pad_ref[:, 0:CONV_HALO, :] = zeros
    pad_ref[:, CONV_HALO + row_len:, :] = zeros
    pad_ref[:, CONV_HALO:CONV_HALO + row_len, :] = y.reshape(nr, row_len, CONV_W)
    span = sh_ref.shape[2]
    for part in range(SUBLANES):
        sh_ref[part] = pad_ref[:, part:part + span, :]
    first = CONV_HALO - CONV_K // 2
    acc = jnp.zeros((nr, row_len, CONV_W), f32)
    for k in range(CONV_K):
        whole, part = divmod(first + k, SUBLANES)
        acc = acc + dw_ref[k:k + 1, :].reshape(1, 1, CONV_W) * sh_ref[part, :, whole * SUBLANES:whole * SUBLANES + row_len, :]
    yc = acc.reshape(tl, CONV_W) + db_ref[...]
    group_mean = lambda v: sum(jnp.dot(p, avg_ref[...], preferred_element_type=f32) for p in _split3(v))
    mu = group_mean(yc)
    cen = yc - mu
    var = group_mean(cen * cen)
    yn = cen * lax.rsqrt(var + EPS) * gw_ref[...] + gb_ref[...]
    o_ref[0] = _silu(yn)


def _conv_module(gl, dw_w, dw_b, gn_w, gn_b, row_len):
    b, l, w2 = gl.shape
    tl = max(row_len, min(CONV_TILE, l))
    gd = CONV_W // CONV_GROUPS
    assert gd & (gd - 1) == 0, "1/group_size must be exact in bf16"
    avg = jnp.asarray(np.kron(np.eye(CONV_GROUPS), np.full((gd, gd), 1.0 / gd)), bf16)
    const = lambda shape: pl.BlockSpec(shape, lambda bi, ji: (0,) * len(shape))
    vec = lambda v: v.reshape(1, CONV_W)
    return pl.pallas_call(
        functools.partial(_conv_kernel, row_len=row_len),
        grid=(b, l // tl),
        in_specs=[pl.BlockSpec((1, tl, w2), lambda bi, ji: (bi, ji, 0)),
                  const((CONV_K, CONV_W)), const((1, CONV_W)), const((1, CONV_W)), const((1, CONV_W)),
                  const((CONV_W, CONV_W))],
        out_specs=pl.BlockSpec((1, tl, CONV_W), lambda bi, ji: (bi, ji, 0)),
        out_shape=jax.ShapeDtypeStruct((b, l, CONV_W), f32),
        scratch_shapes=[pltpu.VMEM((tl // row_len, row_len + 2 * CONV_HALO, CONV_W), f32),
                        pltpu.VMEM((SUBLANES, tl // row_len, row_len + 2 * CONV_HALO - SUBLANES, CONV_W), f32)],
        compiler_params=_params("parallel", "parallel"),
        name="conv_module",
    )(gl, dw_w, vec(dw_b), vec(gn_w), vec(gn_b), avg)


def _mix_out_kernel(of_ref, ob_ref, z_ref, yf_ref, yc_ref, x_ref, gt_ref, gnw_ref, w_ref, o_ref):
    acc = jnp.dot(yf_ref[0].astype(bf16), w_ref[GDN_W:GDN_W + FNET_W, :], preferred_element_type=f32)
    acc = acc + jnp.dot(yc_ref[0].astype(bf16), w_ref[GDN_W + FNET_W:, :], preferred_element_type=f32)
    for h in range(GDN_HEADS):
        sl = slice(h * GDN_DK, (h + 1) * GDN_DK)
        o = of_ref[0, :, sl].astype(f32) + ob_ref[0, :, sl].astype(f32)
        o = o * lax.rsqrt(jnp.mean(o * o, axis=-1, keepdims=True) + EPS) * gnw_ref[...]
        y = (o * _silu(z_ref[0, :, sl].astype(f32))).astype(bf16)
        acc = acc + jnp.dot(y, w_ref[sl, :], preferred_element_type=f32)
    o_ref[0] = x_ref[0] + gt_ref[0] * acc


def _mix_out(o_f, o_b, z, y_f, y_c, x, gate, gdn_norm, w_out_b):
    b, l, d = x.shape
    tm = min(ROW_TILE, l)
    tok = lambda width: pl.BlockSpec((1, tm, width), lambda i, j: (i, j, 0))
    return pl.pallas_call(
        _mix_out_kernel,
        grid=(b, l // tm),
        in_specs=[tok(GDN_W), tok(GDN_W), tok(GDN_W), tok(FNET_W), tok(CONV_W), tok(d),
                  pl.BlockSpec((1, 1, d), lambda i, j: (i, 0, 0)),
                  pl.BlockSpec((1, GDN_DK), lambda i, j: (0, 0)),
                  pl.BlockSpec((d, d), lambda i, j: (0, 0))],
        out_specs=tok(d),
        out_shape=jax.ShapeDtypeStruct((b, l, d), f32),
        compiler_params=_params("parallel", "parallel"),
        name="mix_out_proj",
    )(o_f, o_b, z, y_f, y_c, x, gate, gdn_norm.reshape(1, GDN_DK), w_out_b)


def _peer_fold_kernel(keys_ref, wq_ref, o_ref):
    o_ref[...] = lax.dot_general(keys_ref[0], wq_ref[...], (((1,), (1,)), ((), ())),
                                 preferred_element_type=f32, precision=lax.Precision.HIGHEST).astype(bf16)


def _peer_fold_keys(peer_wq, peer_keys):
    d, nq = peer_wq.shape
    nhp = nq // PEER_DH
    return pl.pallas_call(
        _peer_fold_kernel,
        grid=(nhp,),
        in_specs=[pl.BlockSpec((1, PEER_NKEYS, PEER_DH), lambda i: (i, 0, 0)),
                  pl.BlockSpec((d, PEER_DH), lambda i: (0, i))],
        out_specs=pl.BlockSpec((PEER_NKEYS, d), lambda i: (i, 0)),
        out_shape=jax.ShapeDtypeStruct((nhp * PEER_NKEYS, d), bf16),
        compiler_params=_params("parallel"),
        name="peer_fold_keys",
    )(peer_keys.reshape(nhp, PEER_NKEYS, PEER_DH), peer_wq)


def _peer_query_kernel(x_ref, nw_ref, sc_ref, sh_ref, wk_ref, h_ref, s_ref):
    ht = _ada_norm(x_ref[0], nw_ref[...], sc_ref[0], sh_ref[0]).T.astype(bf16)
    h_ref[0] = ht
    s = jnp.dot(wk_ref[...], ht, preferred_element_type=f32)
    s_ref[0] = s.reshape(s_ref.shape[1:])


def _peer_query(x, nw, sc, sh, wk_b):
    b, l, d = x.shape
    tm = min(PEER_QUERY_TILE, l)
    nhp = 2 * PEER_HEADS
    vec = pl.BlockSpec((1, 1, d), lambda i, j: (i, 0, 0))
    return pl.pallas_call(
        _peer_query_kernel,
        grid=(b, l // tm),
        in_specs=[pl.BlockSpec((1, tm, d), lambda i, j: (i, j, 0)),
                  pl.BlockSpec((1, d), lambda i, j: (0, 0)), vec, vec,
                  pl.BlockSpec((nhp * PEER_NKEYS, d), lambda i, j: (0, 0))],
        out_specs=[pl.BlockSpec((1, d, tm), lambda i, j: (i, 0, j)),
                   pl.BlockSpec((1, nhp, PEER_NKEYS, tm), lambda i, j: (i, 0, 0, j))],
        out_shape=[jax.ShapeDtypeStruct((b, d, l), bf16),
                   jax.ShapeDtypeStruct((b, nhp, PEER_NKEYS, l), f32)],
        compiler_params=_params("parallel", "parallel"),
        name="peer_query",
    )(x, nw.reshape(1, d), sc, sh, wk_b)


def _batcher_pairs(n):
    pairs = []

    def merge(lo, cnt, r):
        m = 2 * r
        if m < cnt:
            merge(lo, cnt, m)
            merge(lo + r, cnt, m)
            pairs.extend((i, i + r) for i in range(lo + r, lo + cnt - r, m))
        else:
            pairs.append((lo, lo + r))

    def sort(lo, cnt):
        if cnt > 1:
            sort(lo, cnt // 2)
            sort(lo + cnt // 2, cnt // 2)
            merge(lo, cnt, 1)

    sort(0, n)
    return pairs


def _bitonic_pairs(n):
    pairs, s = [], n // 2
    while s:
        pairs.extend((i, i + s) for i in range(n) if not i & s)
        s //= 2
    return pairs


SORT16 = _batcher_pairs(PEER_TOPK)
N_CAND = 9
SORT_CAND = [(i, j) for i, j in SORT16 if j < N_CAND]
MERGE16 = _bitonic_pairs(PEER_TOPK)


def _exchange(rows, pairs):
    for i, j in pairs:
        hi, lo = jnp.maximum(rows[i], rows[j]), jnp.minimum(rows[i], rows[j])
        rows[i], rows[j] = hi, lo
    return rows


def _merge_top16(rows, shift):
    other = [pltpu.roll(r, shift, axis=0) for r in rows]
    return _exchange([jnp.maximum(rows[i], other[PEER_TOPK - 1 - i]) for i in range(PEER_TOPK)], MERGE16)


def _sorted_top16(blocks):
    rows = _exchange(list(blocks), SORT16)
    for shift in (4, 2, 1):
        rows = _merge_top16(rows, shift)
    return rows


def _all_sublanes(op, x):
    for shift in (4, 2, 1):
        x = op(x, pltpu.roll(x, shift, axis=0))
    return x


def _kth_largest16(cands):
    rows = _exchange(list(cands), SORT_CAND)
    other = [pltpu.roll(r, 4, axis=0) for r in rows]
    pad = PEER_TOPK - N_CAND
    z = ([rows[i] for i in range(pad)]
         + [jnp.maximum(rows[i], other[PEER_TOPK - 1 - i]) for i in range(pad, N_CAND)]
         + [other[PEER_TOPK - 1 - i] for i in range(N_CAND, PEER_TOPK)])
    z = _merge_top16(_exchange(z, MERGE16), 2)
    other = [pltpu.roll(r, 1, axis=0) for r in z]
    top = [jnp.maximum(z[i], other[PEER_TOPK - 1 - i]) for i in range(PEER_TOPK)]
    return functools.reduce(jnp.minimum, top)


def _candidates(av_rep, av_lo, av_hi, bv_rep, bv_lo, bv_hi, upper):
    return [av_rep[0] * bv_lo, av_rep[0] * bv_hi, av_rep[1] * bv_lo, av_rep[2] * bv_lo, av_rep[3] * bv_lo,
            jnp.where(upper, av_lo * bv_rep[0], 0.0), jnp.where(upper, av_lo * bv_rep[1], 0.0),
            jnp.where(upper, av_lo * bv_rep[2], 0.0), av_hi * bv_rep[0]]


def _peer_topk_kernel(s_ref, a_ref, b_ref, th_ref, sv1_ref, sv2_ref):
    tt = s_ref.shape[-1]
    nblk = PEER_NKEYS // SUBLANES
    half = SUBLANES
    upper = lax.broadcasted_iota(jnp.int32, (SUBLANES, tt), 0) >= SUBLANES // 2

    def head(h, carry):
        blk = lambda p, k: s_ref[0, 2 * h + p, k * SUBLANES:(k + 1) * SUBLANES, :]
        t1 = _sorted_top16([blk(0, k) for k in range(nblk)])
        t2 = _sorted_top16([blk(1, k) for k in range(nblk)])
        for r in range(PEER_TOPK):
            sv1_ref[r:r + 1, :] = t1[r][0:1, :]
            sv2_ref[r:r + 1, :] = t2[r][0:1, :]
        m1, m2 = t1[0], t2[0]
        av_rep = [jnp.exp(t1[r] - m1) for r in range(4)]
        bv_rep = [jnp.exp(t2[r] - m2) for r in range(3)]
        av_lo, av_hi = jnp.exp(sv1_ref[0:half, :] - m1), jnp.exp(sv1_ref[half:, :] - m1)
        bv_lo, bv_hi = jnp.exp(sv2_ref[0:half, :] - m2), jnp.exp(sv2_ref[half:, :] - m2)
        cand = _candidates(av_rep, av_lo, av_hi, bv_rep, bv_lo, bv_hi, upper)
        theta = _kth_largest16(cand)
        sel = [c >= theta for c in cand]
        zsum = _all_sublanes(jnp.add, sum(jnp.where(m, c, 0.0) for m, c in zip(sel, cand)))
        rz = GELU_HALF / zsum
        scaled = _candidates([a * rz for a in av_rep], av_lo * rz, av_hi * rz, bv_rep, bv_lo, bv_hi, upper)
        lowest = functools.reduce(jnp.minimum, [jnp.where(m, c, jnp.inf) for m, c in zip(sel, scaled)])
        th_ref[0, h] = _all_sublanes(jnp.minimum, lowest)[0:1, :]
        for k in range(nblk):
            rows = slice(k * SUBLANES, (k + 1) * SUBLANES)
            x1, x2 = blk(0, k), blk(1, k)
            a_ref[0, h, rows, :] = jnp.where(x1 >= t1[PEER_TOPK - 1], jnp.exp(x1 - m1) * rz, 0.0)
            b_ref[0, h, rows, :] = jnp.where(x2 >= t2[PEER_TOPK - 1], jnp.exp(x2 - m2), 0.0)
        return carry

    lax.fori_loop(0, PEER_HEADS, head, 0)


def _peer_topk(scores):
    b, nhp, nk, l = scores.shape
    tt = min(PEER_TOPK_TILE, l)
    ab_spec = pl.BlockSpec((1, PEER_HEADS, nk, tt), lambda i, j: (i, 0, 0, j))
    return pl.pallas_call(
        _peer_topk_kernel,
        grid=(b, l // tt),
        in_specs=[pl.BlockSpec((1, nhp, nk, tt), lambda i, j: (i, 0, 0, j))],
        out_specs=[ab_spec, ab_spec, pl.BlockSpec((1, PEER_HEADS, 1, tt), lambda i, j: (i, 0, 0, j))],
        out_shape=[jax.ShapeDtypeStruct((b, PEER_HEADS, nk, l), f32)] * 2
                  + [jax.ShapeDtypeStruct((b, PEER_HEADS, 1, l), f32)],
        scratch_shapes=[pltpu.VMEM((PEER_TOPK, tt), f32), pltpu.VMEM((PEER_TOPK, tt), f32)],
        compiler_params=_params("parallel", "parallel"),
        name="peer_topk",
    )(scores)


def _gelu_tanh_doubled(x):
    inner = x * (GELU_C1 + (GELU_C1 * 0.044715) * (x * x))
    return x + x * jnp.tanh(inner)


def _peer_dense_kernel(h_ref, a_ref, b_ref, th_ref, u_ref, un_ref, vt_ref, vp_ref, x_ref, gt_ref, fw_ref, o_ref,
                       acc_ref, sc_ref, sn_ref, s_ref, g_ref, gp_ref, *, final_norm):
    e = pl.program_id(2)
    tt = h_ref.shape[2]
    sub = PEER_SUB_TILE
    te = u_ref.shape[0]
    half = te // 2
    n_sub = te // sub
    keys_per_sub = sub // PEER_NKEYS
    n_first = te // PEER_NKEYS
    lane_chunks = [slice(lc * PEER_LANE_CHUNK, (lc + 1) * PEER_LANE_CHUNK) for lc in range(tt // PEER_LANE_CHUNK)]
    proj_chunks = [slice(lc * PEER_PROJ_CHUNK, (lc + 1) * PEER_PROJ_CHUNK) for lc in range(tt // PEER_PROJ_CHUNK)]
    assert n_sub * keys_per_sub == n_first and n_first >= 4 * len(proj_chunks)

    def scores(u_rows):
        return jnp.dot(u_rows, h_ref[0], preferred_element_type=f32)

    def score_piece(k):
        def run():
            if k < n_sub:
                s_ref[(k - 1) * sub:k * sub, :] = scores(u_ref[k * sub:(k + 1) * sub, :])
            else:
                sn_ref[...] = scores(un_ref[...])
        return run

    def project_piece(v_ref, src_ref, lanes):
        def run():
            acc_ref[:, lanes] += jnp.dot(v_ref[:, 0:half], src_ref[:, lanes], preferred_element_type=f32)
        return run

    pieces = {keys_per_sub * (k - 1): score_piece(k) for k in range(1, n_sub + 1)}
    for j, lanes in enumerate(proj_chunks):
        pieces[1 + keys_per_sub * j] = project_piece(vp_ref, gp_ref, lanes)
        pieces[n_first // 2 + 1 + keys_per_sub * j] = project_piece(vt_ref, g_ref, lanes)

    @pl.when(e == 0)
    def _():
        acc_ref[...] = jnp.zeros_like(acc_ref)
        gp_ref[...] = jnp.zeros_like(gp_ref)
        sc_ref[...] = scores(u_ref[0:sub, :])

    for k in range(n_sub):
        dst, base = (g_ref, k * sub) if k * sub < half else (gp_ref, k * sub - half)
        for r in range(keys_per_sub):
            ii = k * keys_per_sub + r
            if ii in pieces:
                pieces[ii]()
            for jb in range(PEER_NKEYS // PEER_ROW_CHUNK):
                second = slice(jb * PEER_ROW_CHUNK, (jb + 1) * PEER_ROW_CHUNK)
                off = r * PEER_NKEYS + jb * PEER_ROW_CHUNK
                for lanes in lane_chunks:
                    wsum = jnp.zeros((PEER_ROW_CHUNK, PEER_LANE_CHUNK), f32)
                    for h in range(PEER_HEADS):
                        p = a_ref[0, h, ii:ii + 1, lanes] * b_ref[0, h, second, lanes]
                        wsum = wsum + jnp.where(p >= th_ref[0, h, :, lanes], p, 0.0)
                    if k == 0:
                        s = sc_ref[off:off + PEER_ROW_CHUNK, lanes]
                    else:
                        s = s_ref[(k - 1) * sub + off:(k - 1) * sub + off + PEER_ROW_CHUNK, lanes]
                    dst[base + off:base + off + PEER_ROW_CHUNK, lanes] = (
                        _gelu_tanh_doubled(s.astype(bf16)) * wsum.astype(bf16))
    sc_ref[...] = sn_ref[...]

    @pl.when(e == pl.num_programs(2) - 1)
    def _():
        acc = acc_ref[...] + jnp.dot(vt_ref[:, half:], gp_ref[...], preferred_element_type=f32)
        y = x_ref[0] + gt_ref[0] * acc.T
        if final_norm:
            y = y * lax.rsqrt(jnp.mean(y * y, axis=-1, keepdims=True) + EPS) * fw_ref[...]
        o_ref[0] = y


def _peer_dense(hb, a, bsel, theta, u_b, vt_b, x, gate, final_w, final_norm):
    b, l, d = x.shape
    ne = u_b.shape[0]
    tt = min(PEER_TOKEN_TILE, l)
    te = PEER_EXPERT_TILE
    sub = PEER_SUB_TILE
    nfirst = te // PEER_NKEYS
    n_sub = te // sub
    last_sub = ne // sub - 1
    return pl.pallas_call(
        functools.partial(_peer_dense_kernel, final_norm=final_norm),
        grid=(b, l // tt, ne // te),
        in_specs=[pl.BlockSpec((1, d, tt), lambda i, j, e: (i, 0, j)),
                  pl.BlockSpec((1, PEER_HEADS, nfirst, tt), lambda i, j, e: (i, 0, e, j)),
                  pl.BlockSpec((1, PEER_HEADS, PEER_NKEYS, tt), lambda i, j, e: (i, 0, 0, j)),
                  pl.BlockSpec((1, PEER_HEADS, 1, tt), lambda i, j, e: (i, 0, 0, j)),
                  pl.BlockSpec((te, d), lambda i, j, e: (e, 0)),
                  pl.BlockSpec((sub, d), lambda i, j, e: (jnp.minimum((e + 1) * n_sub, last_sub), 0)),
                  pl.BlockSpec((d, te), lambda i, j, e: (0, e)),
                  pl.BlockSpec((d, te // 2), lambda i, j, e: (0, jnp.maximum(2 * e - 1, 0))),
                  pl.BlockSpec((1, tt, d), lambda i, j, e: (i, j, 0)),
                  pl.BlockSpec((1, 1, d), lambda i, j, e: (i, 0, 0)),
                  pl.BlockSpec((1, d), lambda i, j, e: (0, 0))],
        out_specs=pl.BlockSpec((1, tt, d), lambda i, j, e: (i, j, 0)),
        out_shape=jax.ShapeDtypeStruct((b, l, d), f32),
        scratch_shapes=[pltpu.VMEM((d, tt), f32), pltpu.VMEM((sub, tt), f32), pltpu.VMEM((sub, tt), f32),
                        pltpu.VMEM((te - sub, tt), f32),
                        pltpu.VMEM((te // 2, tt), bf16), pltpu.VMEM((te // 2, tt), bf16)],
        compiler_params=_params("parallel", "parallel", "arbitrary"),
        name="peer_dense",
    )(hb, a, bsel, theta, u_b, u_b, vt_b, vt_b, x, gate, final_w.reshape(1, d))


def _gdn(proj, s0):
    q, k, v, gb = proj[:4]
    return _gdn_recur(s0, *_gdn_intra(q, k, v, gb), gb)


def _mixer(x, proj, o_f, o_b, gate, row_len, lw):
    z, f, cgl = proj[4:]
    y_f = _fnet(f)
    y_c = _conv_module(cgl, lw["dw_w"], lw["dw_b"], lw["gn_w"], lw["gn_b"], row_len)
    return _mix_out(o_f, o_b, z, y_f, y_c, x, gate, lw["gdn_norm"], lw["w_out"])


def _peer(x, sc, sh, gate, lw, final_w, final_norm=False):
    hb, scores = _peer_query(x, lw["norm2"], sc, sh, lw["peer_wk"])
    a, bsel, theta = _peer_topk(scores)
    return _peer_dense(hb, a, bsel, theta, lw["peer_u"], lw["peer_vt"], x, gate, final_w, final_norm)


def kernel(x, c, ctx, c_ctx, w_mod, b_mod, norm1, norm2, w_in, conv_qkv, a_log, dt_bias, gdn_norm, dw_w, dw_b,
           gn_w, gn_b, w_out, peer_wq, peer_keys, peer_u, peer_v, final_norm):
    bsz, _, d = x.shape
    depth = w_mod.shape[0]
    cc = jnp.zeros((SUBLANES, d), f32).at[:bsz].set(c).at[bsz].set(c_ctx)
    mods = _modulation(cc, w_mod, b_mod)
    xc = ctx
    for i in range(depth):
        last = i == depth - 1
        lw = {
            "norm2": norm2[i],
            "gdn_norm": gdn_norm[i], "dw_w": dw_w[i], "dw_b": dw_b[i], "gn_w": gn_w[i], "gn_b": gn_b[i],
            "w_out": w_out[i].astype(bf16), "peer_wk": _peer_fold_keys(peer_wq[i], peer_keys[i]),
            "peer_u": peer_u[i].astype(bf16), "peer_vt": peer_v[i].T.astype(bf16),
        }
        w_in_p = _pack_w_in(w_in[i])
        mod = mods[i, :bsz].reshape(bsz, 6, 1, d)
        mod_c = jnp.broadcast_to(mods[i, bsz].reshape(1, 6, 1, d), (bsz, 6, 1, d))
        sh1, sc1, gt1, sh2, sc2, gt2 = (mod[:, j] for j in range(6))
        csh1, csc1, cgt1, csh2, csc2, cgt2 = (mod_c[:, j] for j in range(6))

        gdn_w = (conv_qkv[i], a_log[i], dt_bias[i])
        pc = _norm_in(xc, norm1[i], csc1, csh1, w_in_p, *gdn_w)
        p = _norm_in(x, norm1[i], sc1, sh1, w_in_p, *gdn_w)
        zero = jnp.zeros((2 * bsz * GDN_HEADS, GDN_DK, GDN_DK), f32)
        oc_f, oc_b, s_ctx = _gdn(pc, zero)
        o_f, o_b, _ = _gdn(p, s_ctx)
        x = _mixer(x, p, o_f, o_b, gt1, GRID_W, lw)
        x = _peer(x, sc2, sh2, gt2, lw, final_norm, final_norm=last)
        if not last:
            xc = _mixer(xc, pc, oc_f, oc_b, cgt1, xc.shape[1], lw)
            xc = _peer(xc, csc2, csh2, cgt2, lw, final_norm)
    return x
```
